```python
import jax, jax.numpy as jnp
from jax import lax
import numpy as np

D_MODEL = 1024
BATCH = 4
SEQ = 8192
DEPTH = 2
DEC_BATCH = 128
DEC_SEQ = 8
PAST_LEN = 16384
PAGE_SIZE = 128

SB_HEADS = 4
SB_HEAD_DIM = 64
SB_WIDTH = SB_HEADS * SB_HEAD_DIM
SB_SCALE = SB_HEAD_DIM ** -0.5
CONV_WIDTH = D_MODEL // 4
CONV_K = 31
MLA_HEADS = 8
MLA_NOPE = 64
MLA_ROPE = 32
MLA_V = 64
MLA_Q_RANK = D_MODEL // 4
MLA_KV_RANK = D_MODEL // 8
MLA_SCALE = (MLA_NOPE + MLA_ROPE) ** -0.5
ROPE_BASE = 10000.0
N_BRANCH = 3
FF_HIDDEN = 4 * D_MODEL
Q_BLOCK = 128
EPS = 1e-6
IN_SPLITS = (SB_WIDTH, SB_WIDTH, SB_WIDTH, 2 * CONV_WIDTH, MLA_Q_RANK, MLA_KV_RANK, MLA_ROPE, N_BRANCH * D_MODEL)
IN_COLS = 3 * SB_WIDTH + 2 * CONV_WIDTH + MLA_Q_RANK + MLA_KV_RANK + MLA_ROPE + N_BRANCH * D_MODEL

kernel_name = "hybrid_sb_conformer_mla_decoder_step"


def rmsnorm(x, g):
    xf = x.astype(jnp.float32)
    y = xf * lax.rsqrt(jnp.mean(jnp.square(xf), axis=-1, keepdims=True) + EPS)
    return (y * g.astype(jnp.float32)).astype(x.dtype)


def layernorm(x, g, b):
    xf = x.astype(jnp.float32)
    mu = jnp.mean(xf, axis=-1, keepdims=True)
    var = jnp.mean(jnp.square(xf - mu), axis=-1, keepdims=True)
    y = (xf - mu) * lax.rsqrt(var + EPS)
    return (y * g.astype(jnp.float32) + b.astype(jnp.float32)).astype(x.dtype)


def rope(x, pos):
    half = x.shape[-1] // 2
    inv = ROPE_BASE ** (-jnp.arange(half, dtype=jnp.float32) / half)
    ang = pos.astype(jnp.float32)[:, None] * inv[None, :]
    cos = jnp.cos(ang)[None, :, None, :]
    sin = jnp.sin(ang)[None, :, None, :]
    xf = x.astype(jnp.float32)
    x1, x2 = xf[..., :half], xf[..., half:]
    return jnp.concatenate([x1 * cos - x2 * sin, x2 * cos + x1 * sin], axis=-1).astype(x.dtype)


def split_blocks(t):
    B, S = t.shape[:2]
    return t.reshape((B, S // Q_BLOCK, Q_BLOCK) + t.shape[2:]).swapaxes(0, 1)


def merge_blocks(t):
    nb, B, Q = t.shape[:3]
    return t.swapaxes(0, 1).reshape((B, nb * Q) + t.shape[3:])


def sb_weights(z, mask, carry):
    l = jnp.where(mask, jax.nn.log_sigmoid(-z), 0.0)
    e = carry[..., None] + lax.cumsum(l, axis=3, reverse=True) - l
    w = jnp.where(mask, jnp.exp(jax.nn.log_sigmoid(z) + e), 0.0)
    return w, carry + jnp.sum(l, axis=-1)


def sb_prompt(q, k, v):
    S = q.shape[1]
    kpos = jnp.arange(S)

    def one(args):
        qi, i = args
        z = jnp.einsum('bqhd,bkhd->bhqk', qi, k, preferred_element_type=jnp.float32) * SB_SCALE
        qpos = i * Q_BLOCK + jnp.arange(Q_BLOCK)
        mask = kpos[None, :] < qpos[:, None]
        w, _ = sb_weights(z, mask, jnp.zeros(z.shape[:3], jnp.float32))
        return jnp.einsum('bhqk,bkhd->bqhd', w.astype(v.dtype), v, preferred_element_type=jnp.float32)

    return merge_blocks(lax.map(one, (split_blocks(q), jnp.arange(S // Q_BLOCK))))


def sb_sample(q, k, v, cache_k, cache_v, layer, page_table):
    T = q.shape[1]
    z = jnp.einsum('bqhd,bkhd->bhqk', q, k, preferred_element_type=jnp.float32) * SB_SCALE
    mask = jnp.arange(T)[None, :] < jnp.arange(T)[:, None]
    w, carry = sb_weights(z, mask, jnp.zeros(z.shape[:3], jnp.float32))
    acc = jnp.einsum('bhqk,bkhd->bqhd', w.astype(v.dtype), v, preferred_element_type=jnp.float32)

    def step(st, pt):
        carry, acc = st
        kp = cache_k[layer, pt]
        vp = cache_v[layer, pt]
        z = jnp.einsum('bqhd,bkhd->bhqk', q, kp, preferred_element_type=jnp.float32) * SB_SCALE
        w, carry = sb_weights(z, True, carry)
        acc = acc + jnp.einsum('bhqk,bkhd->bqhd', w.astype(vp.dtype), vp, preferred_element_type=jnp.float32)
        return (carry, acc), None

    (_, acc), _ = lax.scan(step, (carry, acc), page_table.T[::-1])
    return acc


def mla_prompt(q_nope, q_pe, k_nope, k_pe, v):
    S = q_nope.shape[1]
    kpos = jnp.arange(S)

    def one(args):
        qn, qp, i = args
        s = (jnp.einsum('bqhd,bkhd->bhqk', qn, k_nope, preferred_element_type=jnp.float32)
             + jnp.einsum('bqhr,bkr->bhqk', qp, k_pe, preferred_element_type=jnp.float32)) * MLA_SCALE
        qpos = i * Q_BLOCK + jnp.arange(Q_BLOCK)
        s = jnp.where(kpos[None, :] <= qpos[:, None], s, -jnp.inf)
        p = jax.nn.softmax(s, axis=-1)
        return jnp.einsum('bhqk,bkhd->bqhd', p.astype(v.dtype), v, preferred_element_type=jnp.float32)

    return merge_blocks(lax.map(one, (split_blocks(q_nope), split_blocks(q_pe), jnp.arange(S // Q_BLOCK))))


def mla_sample(q_nope, q_pe, ckv, kpe, w_uk, w_uv, cache_ckv, cache_kpe, layer, page_table):
    T = q_nope.shape[1]
    q_lat = jnp.einsum('bqhd,chd->bqhc', q_nope, w_uk, preferred_element_type=jnp.float32)

    def scores(c_rows, r_rows):
        return (jnp.einsum('bqhc,bkc->bhqk', q_lat, c_rows.astype(jnp.float32))
                + jnp.einsum('bqhr,bkr->bhqk', q_pe, r_rows, preferred_element_type=jnp.float32)) * MLA_SCALE

    s = scores(ckv, kpe)
    s = jnp.where(jnp.arange(T)[None, :] <= jnp.arange(T)[:, None], s, -jnp.inf)
    m = jnp.max(s, axis=-1)
    p = jnp.exp(s - m[..., None])
    l = jnp.sum(p, axis=-1)
    acc = jnp.einsum('bhqk,bkc->bhqc', p, ckv.astype(jnp.float32))

    def step(st, pt):
        m, l, acc = st
        cp = cache_ckv[layer, pt]
        rp = cache_kpe[layer, pt]
        s = scores(cp, rp)
        m_new = jnp.maximum(m, jnp.max(s, axis=-1))
        a = jnp.exp(m - m_new)
        p = jnp.exp(s - m_new[..., None])
        l = l * a + jnp.sum(p, axis=-1)
        acc = acc * a[..., None] + jnp.einsum('bhqk,bkc->bhqc', p, cp.astype(jnp.float32))
        return (m_new, l, acc), None

    (m, l, acc), _ = lax.scan(step, (m, l, acc), page_table.T)
    lat = acc / l[..., None]
    return jnp.einsum('bhqc,chd->bqhd', lat, w_uv.astype(jnp.float32))


def conv_module(u, buf, w_dw, b_dw, g, b, w_o):
    full = jnp.concatenate([buf.astype(u.dtype), u], axis=1)
    y = lax.conv_general_dilated(full, w_dw[:, None, :].astype(u.dtype), window_strides=(1,), padding='VALID',
                                 dimension_numbers=('NWC', 'WIO', 'NWC'), feature_group_count=CONV_WIDTH)
    y = layernorm(y + b_dw, g, b)
    return jax.nn.silu(y) @ w_o, full[:, -(CONV_K - 1):, :]


def layer(x, c, pos, w, cache):
    B, S, _ = x.shape
    mod = jax.nn.silu(c) @ w['w_ada'] + w['b_ada']
    sh1, sc1, g1, sh2, sc2, g2 = jnp.split(mod[:, None, :], 6, axis=-1)
    h = rmsnorm(x, w['norm1_g']) * (1 + sc1) + sh1
    proj = h @ w['w_in']
    offs = np.cumsum(IN_SPLITS)[:-1].tolist()
    qa, ka, va, glu, cq, ckv, kpe, gates = jnp.split(proj, offs, axis=-1)
    qa = qa.reshape(B, S, SB_HEADS, SB_HEAD_DIM)
    ka = ka.reshape(B, S, SB_HEADS, SB_HEAD_DIM)
    va = va.reshape(B, S, SB_HEADS, SB_HEAD_DIM)
    u = glu[..., :CONV_WIDTH] * jax.nn.sigmoid(glu[..., CONV_WIDTH:])
    q = (rmsnorm(cq, w['q_norm_g']) @ w['w_uq']).reshape(B, S, MLA_HEADS, MLA_NOPE + MLA_ROPE)
    q_nope, q_pe = q[..., :MLA_NOPE], rope(q[..., MLA_NOPE:], pos)
    ckv = rmsnorm(ckv, w['kv_norm_g'])
    kpe = rope(kpe[:, :, None, :], pos)[:, :, 0, :]
    if cache is None:
        ya = sb_prompt(qa, ka, va)
        k_nope = jnp.einsum('bkc,chd->bkhd', ckv, w['w_uk'])
        v_c = jnp.einsum('bkc,chd->bkhd', ckv, w['w_uv'])
        yc = mla_prompt(q_nope, q_pe, k_nope, kpe, v_c)
        buf = jnp.zeros((B, CONV_K - 1, CONV_WIDTH), u.dtype)
    else:
        li, c_k, c_v, c_ckv, c_kpe, buf, page_table = cache
        ya = sb_sample(qa, ka, va, c_k, c_v, li, page_table)
        yc = mla_sample(q_nope, q_pe, ckv, kpe, w['w_uk'], w['w_uv'], c_ckv, c_kpe, li, page_table)
    ya = ya.reshape(B, S, SB_WIDTH).astype(x.dtype)
    yc = yc.reshape(B, S, MLA_HEADS * MLA_V).astype(x.dtype)
    yb, new_buf = conv_module(u, buf, w['w_dw'], w['b_dw'], w['cln_g'], w['cln_b'], w['w_o_conv'])
    ga, gb, gc = jnp.split(jax.nn.sigmoid(gates), N_BRANCH, axis=-1)
    merged = ga * (ya @ w['w_o_sb']) + gb * yb + gc * (yc @ w['w_o_mla'])
    x = x + g1 * (merged @ w['w_out'])
    h2 = rmsnorm(x, w['norm2_g']) * (1 + sc2) + sh2
    x = x + g2 * (jnp.square(jax.nn.relu(h2 @ w['w_ff1'])) @ w['w_ff2'])
    return x, (ka, va, ckv, kpe, new_buf)


def setup_inputs(seed: int = 0) -> dict:
    key = jax.random.key(seed)
    ks = jax.random.split(key, 40)
    f32 = jnp.float32
    n_pages = PAST_LEN // PAGE_SIZE
    n_used = DEC_BATCH * n_pages
    n_pool = n_used + n_used // 4

    def nrm(k, shape, scale=1.0):
        return jax.random.normal(k, shape, f32) * scale

    page_table = jax.random.permutation(ks[0], n_pool)[:n_used].reshape(DEC_BATCH, n_pages).astype(jnp.int32)
    return {
        "x_prompt": nrm(ks[1], (BATCH, SEQ, D_MODEL)),
        "x_sample": nrm(ks[2], (DEC_BATCH, DEC_SEQ, D_MODEL)),
        "c_prompt": nrm(ks[3], (BATCH, D_MODEL)),
        "c_sample": nrm(ks[4], (DEC_BATCH, D_MODEL)),
        "cache_sb_k": nrm(ks[5], (DEPTH, n_pool, PAGE_SIZE, SB_HEADS, SB_HEAD_DIM)),
        "cache_sb_v": nrm(ks[6], (DEPTH, n_pool, PAGE_SIZE, SB_HEADS, SB_HEAD_DIM)),
        "cache_mla_ckv": nrm(ks[7], (DEPTH, n_pool, PAGE_SIZE, MLA_KV_RANK)),
        "cache_mla_kpe": nrm(ks[8], (DEPTH, n_pool, PAGE_SIZE, MLA_ROPE)),
        "state_conv": nrm(ks[9], (DEPTH, DEC_BATCH, CONV_K - 1, CONV_WIDTH), 0.5),
        "page_table": page_table,
        "norm1_g": 1.0 + nrm(ks[10], (DEPTH, D_MODEL), 0.01),
        "norm2_g": 1.0 + nrm(ks[11], (DEPTH, D_MODEL), 0.01),
        "final_g": 1.0 + nrm(ks[12], (D_MODEL,), 0.01),
        "w_ada": nrm(ks[13], (DEPTH, D_MODEL, 6 * D_MODEL), 0.5 * D_MODEL ** -0.5),
        "b_ada": nrm(ks[14], (DEPTH, 6 * D_MODEL), 0.01),
        "w_in": nrm(ks[15], (DEPTH, D_MODEL, IN_COLS), D_MODEL ** -0.5),
        "w_o_sb": nrm(ks[16], (DEPTH, SB_WIDTH, D_MODEL), SB_WIDTH ** -0.5),
        "w_dw": nrm(ks[17], (DEPTH, CONV_K, CONV_WIDTH), CONV_K ** -0.5),
        "b_dw": nrm(ks[18], (DEPTH, CONV_WIDTH), 0.01),
        "cln_g": 1.0 + nrm(ks[19], (DEPTH, CONV_WIDTH), 0.01),
        "cln_b": nrm(ks[20], (DEPTH, CONV_WIDTH), 0.01),
        "w_o_conv": nrm(ks[21], (DEPTH, CONV_WIDTH, D_MODEL), CONV_WIDTH ** -0.5),
        "q_norm_g": 1.0 + nrm(ks[22], (DEPTH, MLA_Q_RANK), 0.01),
        "w_uq": nrm(ks[23], (DEPTH, MLA_Q_RANK, MLA_HEADS * (MLA_NOPE + MLA_ROPE)), MLA_Q_RANK ** -0.5),
        "kv_norm_g": 1.0 + nrm(ks[24], (DEPTH, MLA_KV_RANK), 0.01),
        "w_uk": nrm(ks[25], (DEPTH, MLA_KV_RANK, MLA_HEADS, MLA_NOPE), MLA_KV_RANK ** -0.5),
        "w_uv": nrm(ks[26], (DEPTH, MLA_KV_RANK, MLA_HEADS, MLA_V), MLA_KV_RANK ** -0.5),
        "w_o_mla": nrm(ks[27], (DEPTH, MLA_HEADS * MLA_V, D_MODEL), (MLA_HEADS * MLA_V) ** -0.5),
        "w_out": nrm(ks[28], (DEPTH, D_MODEL, D_MODEL), D_MODEL ** -0.5),
        "w_ff1": nrm(ks[29], (DEPTH, D_MODEL, FF_HIDDEN), D_MODEL ** -0.5),
        "w_ff2": nrm(ks[30], (DEPTH, FF_HIDDEN, D_MODEL), FF_HIDDEN ** -0.5),
    }


def reference(x_prompt, x_sample, c_prompt, c_sample, cache_sb_k, cache_sb_v, cache_mla_ckv, cache_mla_kpe,
              state_conv, page_table, norm1_g, norm2_g, final_g, w_ada, b_ada, w_in, w_o_sb, w_dw, b_dw,
              cln_g, cln_b, w_o_conv, q_norm_g, w_uq, kv_norm_g, w_uk, w_uv, w_o_mla, w_out, w_ff1, w_ff2):
    pos_p = jnp.arange(x_prompt.shape[1], dtype=jnp.int32)
    pos_s = PAST_LEN + jnp.arange(x_sample.shape[1], dtype=jnp.int32)
    xp, xs = x_prompt, x_sample
    rows_p, rows_s = [], []
    for l in range(DEPTH):
        w = {
            "norm1_g": norm1_g[l], "norm2_g": norm2_g[l], "w_ada": w_ada[l], "b_ada": b_ada[l],
            "w_in": w_in[l], "w_o_sb": w_o_sb[l], "w_dw": w_dw[l], "b_dw": b_dw[l],
            "cln_g": cln_g[l], "cln_b": cln_b[l], "w_o_conv": w_o_conv[l], "q_norm_g": q_norm_g[l],
            "w_uq": w_uq[l], "kv_norm_g": kv_norm_g[l], "w_uk": w_uk[l], "w_uv": w_uv[l],
            "w_o_mla": w_o_mla[l], "w_out": w_out[l], "w_ff1": w_ff1[l], "w_ff2": w_ff2[l],
        }
        xp, rp = layer(xp, c_prompt, pos_p, w, None)
        xs, rs = layer(xs, c_sample, pos_s, w,
                       (l, cache_sb_k, cache_sb_v, cache_mla_ckv, cache_mla_kpe, state_conv[l], page_table))
        rows_p.append(rp)
        rows_s.append(rs)
    y_prompt = rmsnorm(xp, final_g)
    y_sample = rmsnorm(xs, final_g)
    stk = lambda rows, i: jnp.stack([r[i] for r in rows], axis=0)
    return (y_prompt, y_sample,
            stk(rows_p, 0), stk(rows_p, 1), stk(rows_p, 2), stk(rows_p, 3), stk(rows_p, 4),
            stk(rows_s, 0), stk(rows_s, 1), stk(rows_s, 2), stk(rows_s, 3), stk(rows_s, 4))
```

```python
import functools

import jax
import jax.numpy as jnp
from jax import lax
from jax.experimental import pallas as pl
from jax.experimental.pallas import tpu as pltpu

F32 = jnp.float32
BF16 = jnp.bfloat16

EPS = 1e-6
ROPE_BASE = 10000.0
LANES = 128
CONV_HALO = 32
VMEM_LIMIT = 56 * 1024 * 1024


def _cparams(sem):
    return pltpu.CompilerParams(dimension_semantics=sem, vmem_limit_bytes=VMEM_LIMIT)


def _dot(a, b):
    return jnp.dot(a, b, preferred_element_type=F32)


def _dot_nt(a, b):
    return lax.dot_general(a, b, (((1,), (1,)), ((), ())), preferred_element_type=F32)


def _rms(x, g):
    return x * lax.rsqrt(jnp.mean(x * x, axis=-1, keepdims=True) + EPS) * g


def _sigmoid(x):
    return 1.0 / (1.0 + jnp.exp(-x))


def _neg_softplus(z):
    return -(jnp.maximum(z, 0.0) + jnp.log(1.0 + jnp.exp(-jnp.abs(z))))


def _suffix_matrix(n):
    r = lax.broadcasted_iota(jnp.int32, (n, n), 0)
    c = lax.broadcasted_iota(jnp.int32, (n, n), 1)
    return jnp.where(r > c, 1.0, 0.0).astype(BF16)


def _suffix_sum(l, u):
    hi = l.astype(BF16)
    lo = (l - hi.astype(F32)).astype(BF16)
    return _dot(hi, u) + _dot(lo, u)


def _mod_kernel(c_ref, w_ref, b_ref, o_ref):
    c = c_ref[...]
    a = (c * _sigmoid(c)).astype(BF16)
    o_ref[...] = _dot(a, w_ref[...]) + b_ref[...]


def _mod_call(c_all, w_ada, b_ada):
    depth, d, n = w_ada.shape
    rows = c_all.shape[0]
    tn = 1024
    return pl.pallas_call(
        _mod_kernel,
        out_shape=jax.ShapeDtypeStruct((depth, rows, n), F32),
        grid=(depth, n // tn),
        in_specs=[
            pl.BlockSpec((rows, d), lambda l, j: (0, 0)),
            pl.BlockSpec((None, d, tn), lambda l, j: (l, 0, j)),
            pl.BlockSpec((None, 1, tn), lambda l, j: (l, 0, j)),
        ],
        out_specs=pl.BlockSpec((None, rows, tn), lambda l, j: (l, 0, j)),
        compiler_params=_cparams(("arbitrary", "arbitrary")),
        name="ada_mod",
    )(c_all, w_ada, b_ada.reshape(depth, 1, n))


def _mod_spec(per_row, tm, d, k, tiles_per_seq):
    if per_row:
        return pl.BlockSpec((tm, d), lambda i: (i, k))
    return pl.BlockSpec((None, 1, d), lambda i: (i // tiles_per_seq, 0, k))


def _const_spec(shape):
    nd = len(shape)
    return pl.BlockSpec(shape, lambda i: (0,) * nd)


def _in_proj_kernel(dims, prompt, *refs):
    sbw, cw, qr, kvr, rope, sb_scale = dims
    if prompt:
        (x_ref, sh_ref, sc_ref, g1_ref, w1_ref, qg_ref, wqa_ref, wqb_ref, kvg_ref, tc_ref, ts_ref, kc_ref,
         ks_ref, wk_ref, ek_ref, wv_ref,
         qab_ref, kab_ref, vab_ref, ka_ref, va_ref, u_ref, q_ref, kf_ref, vc_ref, ckv_ref, kpe_ref) = refs
    else:
        (x_ref, sh_ref, sc_ref, g1_ref, w1_ref, qg_ref, wqa_ref, wqb_ref, kvg_ref, tc_ref, ts_ref, kc_ref,
         ks_ref, wabs_ref,
         qab_ref, ka_ref, va_ref, u_ref, qlat_ref, qpe_ref, ckv_ref, kpe_ref) = refs

    x = x_ref[...]
    h = _rms(x, g1_ref[...]) * (1.0 + sc_ref[...]) + sh_ref[...]
    p = _dot(h.astype(BF16), w1_ref[...])
    o = 0
    qa = p[:, o:o + sbw]; o += sbw
    ka = p[:, o:o + sbw]; o += sbw
    va = p[:, o:o + sbw]; o += sbw
    ga = p[:, o:o + cw]; o += cw
    gb = p[:, o:o + cw]; o += cw
    cq = p[:, o:o + qr]; o += qr
    ckv = p[:, o:o + kvr]; o += kvr
    kpe = p[:, o:o + LANES][:, :rope]; o += LANES
    kpe_rot = p[:, o:o + LANES][:, :rope]

    qab_ref[...] = (qa * sb_scale).astype(BF16)
    ka_ref[...] = ka
    va_ref[...] = va
    u_ref[...] = ga * _sigmoid(gb)

    cqn = _rms(cq, qg_ref[...]).astype(BF16)
    q = _dot(cqn, wqa_ref[...]) * tc_ref[...] + _dot(cqn, wqb_ref[...]) * ts_ref[...]
    ckvn = _rms(ckv, kvg_ref[...])
    kper = kpe * kc_ref[...] + kpe_rot * ks_ref[...]
    ckv_ref[...] = ckvn
    kpe_ref[...] = kper
    if prompt:
        kab_ref[...] = ka.astype(BF16)
        vab_ref[...] = va.astype(BF16)
        q_ref[...] = q.astype(BF16)
        cb = ckvn.astype(BF16)
        kf_ref[...] = (_dot(cb, wk_ref[...]) + _dot(kper.astype(BF16), ek_ref[...])).astype(BF16)
        vc_ref[...] = _dot(cb, wv_ref[...]).astype(BF16)
    else:
        nope_w = wabs_ref.shape[0]
        qlat_ref[...] = _dot(q[:, :nope_w].astype(BF16), wabs_ref[...]).astype(BF16)
        qpe_ref[...] = q[:, nope_w:].astype(BF16)


def _in_proj_call(x, mod, g1, lw, tabs, dims, prompt, tm, tiles_per_seq):
    rows, d = x.shape
    sbw, cw, qr, kvr, rope, _ = dims
    tc, ts, kc, ks = tabs
    ntab = tc.shape[0] // tm
    wq = tc.shape[1]
    per_row = not prompt
    row = lambda w: pl.BlockSpec((tm, w), lambda i: (i, 0))
    tab = lambda w: pl.BlockSpec((tm, w), lambda i: (i % ntab, 0))
    in_specs = [
        row(d),
        _mod_spec(per_row, tm, d, 0, tiles_per_seq),
        _mod_spec(per_row, tm, d, 1, tiles_per_seq),
        _const_spec((1, d)),
        _const_spec(lw["w1"].shape),
        _const_spec((1, qr)),
        _const_spec(lw["wqa"].shape),
        _const_spec(lw["wqb"].shape),
        _const_spec((1, kvr)),
        tab(wq), tab(wq), tab(rope), tab(rope),
    ]
    args = [x, mod, mod, g1, lw["w1"], lw["qg"], lw["wqa"], lw["wqb"], lw["kvg"], tc, ts, kc, ks]
    sds = jax.ShapeDtypeStruct
    if prompt:
        in_specs += [_const_spec(lw["wk"].shape), _const_spec(lw["ek"].shape), _const_spec(lw["wv"].shape)]
        args += [lw["wk"], lw["ek"], lw["wv"]]
        kfw, vcw = lw["wk"].shape[1], lw["wv"].shape[1]
        out_shape = [sds((rows, sbw), BF16)] * 3 + [sds((rows, sbw), F32)] * 2 + [
            sds((rows, cw), F32), sds((rows, wq), BF16), sds((rows, kfw), BF16), sds((rows, vcw), BF16),
            sds((rows, kvr), F32), sds((rows, rope), F32)]
        out_specs = [row(sbw)] * 5 + [row(cw), row(wq), row(kfw), row(vcw), row(kvr), row(rope)]
    else:
        in_specs += [_const_spec(lw["wabs"].shape)]
        args += [lw["wabs"]]
        nope_w, latw = lw["wabs"].shape
        out_shape = [sds((rows, sbw), BF16)] + [sds((rows, sbw), F32)] * 2 + [
            sds((rows, cw), F32), sds((rows, latw), BF16), sds((rows, wq - nope_w), BF16),
            sds((rows, kvr), F32), sds((rows, rope), F32)]
        out_specs = [row(sbw)] * 3 + [row(cw), row(latw), row(wq - nope_w), row(kvr), row(rope)]
    return pl.pallas_call(
        functools.partial(_in_proj_kernel, dims, prompt),
        out_shape=out_shape,
        grid=(rows // tm,),
        in_specs=in_specs,
        out_specs=out_specs,
        compiler_params=_cparams(("arbitrary",)),
        name="in_proj_prompt" if prompt else "in_proj_sample",
    )(*args)


def _sb_prompt_kernel(tq, q_ref, k_ref, v_ref, o_ref):
    i = pl.program_id(2)
    half = q_ref.shape[1] // 2
    q = q_ref[...]
    lane = lax.broadcasted_iota(jnp.int32, (1, q.shape[1]), 1)
    first = lane < half
    zero = jnp.zeros_like(q)
    qh = (jnp.where(first, q, zero), jnp.where(first, zero, q))
    u = _suffix_matrix(tq)
    row = lax.broadcasted_iota(jnp.int32, (tq, tq), 0)
    col = lax.broadcasted_iota(jnp.int32, (tq, tq), 1)
    causal = col < row

    def block(j, state, masked):
        kb = k_ref[pl.ds(pl.multiple_of(j * tq, tq), tq), :]
        vb = v_ref[pl.ds(pl.multiple_of(j * tq, tq), tq), :]
        new = []
        for h in range(2):
            carry, acc = state[h]
            z = _dot_nt(qh[h], kb)
            l = _neg_softplus(z)
            if masked:
                l = jnp.where(causal, l, 0.0)
            w = jnp.exp(z + l + carry + _suffix_sum(l, u))
            if masked:
                w = jnp.where(causal, w, 0.0)
            carry = carry + jnp.sum(l, axis=-1, keepdims=True)
            acc = acc + _dot(w.astype(BF16), vb)
            new.append((carry, acc))
        return tuple(new)

    init = tuple((jnp.zeros((tq, 1), F32), jnp.zeros((tq, q.shape[1]), F32)) for _ in range(2))
    state = block(i, init, True)
    state = lax.fori_loop(0, i, lambda t, s: block(i - 1 - t, s, False), state)
    o_ref[...] = jnp.where(first, state[0][1], state[1][1]).astype(o_ref.dtype)


def _sb_prompt_call(q, k, v, tq):
    b, s, w = q.shape
    pw = LANES
    npair = w // pw
    return pl.pallas_call(
        functools.partial(_sb_prompt_kernel, tq),
        out_shape=jax.ShapeDtypeStruct((b, s, w), BF16),
        grid=(b, npair, s // tq),
        in_specs=[
            pl.BlockSpec((None, tq, pw), lambda bi, hp, i: (bi, i, hp)),
            pl.BlockSpec((None, s, pw), lambda bi, hp, i: (bi, 0, hp)),
            pl.BlockSpec((None, s, pw), lambda bi, hp, i: (bi, 0, hp)),
        ],
        out_specs=pl.BlockSpec((None, tq, pw), lambda bi, hp, i: (bi, i, hp)),
        compiler_params=_cparams(("arbitrary", "arbitrary", "arbitrary")),
        name="sb_prompt",
    )(q, k, v)


def _mla_prompt_kernel(tq, scale, q_ref, k_ref, v_ref, o_ref):
    i = pl.program_id(2)
    vw = v_ref.shape[1]
    lane = lax.broadcasted_iota(jnp.int32, (1, vw), 1)
    first = lane < vw // 2
    row = lax.broadcasted_iota(jnp.int32, (tq, tq), 0)
    col = lax.broadcasted_iota(jnp.int32, (tq, tq), 1)
    causal = col <= row
    qh = (q_ref[:, :LANES], q_ref[:, LANES:])

    def block(j, state, masked):
        start = pl.multiple_of(j * tq, tq)
        vb = v_ref[pl.ds(start, tq), :]
        new = []
        for h in range(2):
            m, l, acc = state[h]
            kb = k_ref[pl.ds(start, tq), h * LANES:(h + 1) * LANES]
            s = _dot_nt(qh[h], kb) * scale
            if masked:
                s = jnp.where(causal, s, -jnp.inf)
            m_new = jnp.maximum(m, jnp.max(s, axis=-1, keepdims=True))
            a = jnp.exp(m - m_new)
            p = jnp.exp(s - m_new)
            l = l * a + jnp.sum(p, axis=-1, keepdims=True)
            acc = acc * a + _dot(p.astype(BF16), vb)
            new.append((m_new, l, acc))
        return tuple(new)

    init = tuple((jnp.full((tq, 1), -jnp.inf, F32), jnp.zeros((tq, 1), F32), jnp.zeros((tq, vw), F32))
                 for _ in range(2))
    state = block(i, init, True)
    state = lax.fori_loop(0, i, lambda t, s: block(i - 1 - t, s, False), state)
    out = jnp.where(first, state[0][2] / state[0][1], state[1][2] / state[1][1])
    o_ref[...] = out.astype(o_ref.dtype)


def _mla_prompt_call(q, k, v, tq, scale):
    b, s, qw = q.shape
    vw = v.shape[2]
    npair = qw // (2 * LANES)
    pv = vw // npair
    return pl.pallas_call(
        functools.partial(_mla_prompt_kernel, tq, scale),
        out_shape=jax.ShapeDtypeStruct((b, s, vw), BF16),
        grid=(b, npair, s // tq),
        in_specs=[
            pl.BlockSpec((None, tq, 2 * LANES), lambda bi, hp, i: (bi, i, hp)),
            pl.BlockSpec((None, s, 2 * LANES), lambda bi, hp, i: (bi, 0, hp)),
            pl.BlockSpec((None, s, pv), lambda bi, hp, i: (bi, 0, hp)),
        ],
        out_specs=pl.BlockSpec((None, tq, pv), lambda bi, hp, i: (bi, i, hp)),
        compiler_params=_cparams(("arbitrary", "arbitrary", "arbitrary")),
        name="mla_prompt",
    )(q, k, v)


def _sample_attn_kernel(cfg, pt_ref, *refs):
    g_pages, n_sb_heads, n_mla_heads, t_new, mla_scale = cfg
    (qm_ref, ql_ref, qp_ref, kn_ref, vn_ref, cn_ref, rn_ref, wuv_ref) = refs[:8]
    pages = refs[8:8 + 4 * g_pages]
    ya_ref, yc_ref = refs[8 + 4 * g_pages:10 + 4 * g_pages]
    carry_ref, acc_ref, m_ref, l_ref, lat_ref = refs[10 + 4 * g_pages:]
    c = pl.program_id(1)
    page = kn_ref.shape[0]
    u = _suffix_matrix(page)
    qm = qm_ref[...]
    ql = ql_ref[...]
    qp = qp_ref[...]

    def sb_block(kb, vb, masked):
        z = _dot_nt(qm, kb)
        l = _neg_softplus(z)
        if masked:
            qi = lax.broadcasted_iota(jnp.int32, z.shape, 0) % t_new
            ki = lax.broadcasted_iota(jnp.int32, z.shape, 1)
            valid = ki < qi
            l = jnp.where(valid, l, 0.0)
        w = jnp.exp(z + l + carry_ref[...] + _suffix_sum(l, u))
        if masked:
            w = jnp.where(valid, w, 0.0)
        carry_ref[...] = carry_ref[...] + jnp.sum(l, axis=-1, keepdims=True)
        acc_ref[...] = acc_ref[...] + _dot(w.astype(BF16), vb)

    def mla_block(cb, rb, masked):
        s = (_dot_nt(ql, cb) + _dot_nt(qp, rb)) * mla_scale
        if masked:
            qi = lax.broadcasted_iota(jnp.int32, s.shape, 0) % t_new
            ki = lax.broadcasted_iota(jnp.int32, s.shape, 1)
            s = jnp.where(ki <= qi, s, -jnp.inf)
        m = m_ref[...]
        m_new = jnp.maximum(m, jnp.max(s, axis=-1, keepdims=True))
        a = jnp.exp(m - m_new)
        p = jnp.exp(s - m_new)
        l_ref[...] = l_ref[...] * a + jnp.sum(p, axis=-1, keepdims=True)
        lat_ref[...] = lat_ref[...] * a + _dot(p.astype(BF16), cb)
        m_ref[...] = m_new

    @pl.when(c == 0)
    def _():
        carry_ref[...] = jnp.zeros_like(carry_ref)
        acc_ref[...] = jnp.zeros_like(acc_ref)
        m_ref[...] = jnp.full_like(m_ref, -jnp.inf)
        l_ref[...] = jnp.zeros_like(l_ref)
        lat_ref[...] = jnp.zeros_like(lat_ref)
        sb_block(kn_ref[...], vn_ref[...], True)
        mla_block(cn_ref[...], rn_ref[...], True)

    for g in range(g_pages):
        k_ref, v_ref, c_ref, r_ref = pages[4 * g:4 * g + 4]
        sb_block(k_ref[...].astype(BF16), v_ref[...].astype(BF16), False)
        mla_block(c_ref[...].astype(BF16), r_ref[...].astype(BF16), False)

    @pl.when(c == pl.num_programs(1) - 1)
    def _():
        acc = acc_ref[...]
        w = acc.shape[1]
        hsel = (lax.broadcasted_iota(jnp.int32, acc.shape, 1) // (w // n_sb_heads)
                == lax.broadcasted_iota(jnp.int32, acc.shape, 0) // t_new)
        ya_ref[...] = jnp.sum(jnp.where(hsel, acc, 0.0).reshape(n_sb_heads, t_new, w), axis=0)
        lat = (lat_ref[...] / l_ref[...]).astype(BF16)
        full = _dot(lat, wuv_ref[...])
        vw = full.shape[1]
        msel = (lax.broadcasted_iota(jnp.int32, full.shape, 1) // (vw // n_mla_heads)
                == lax.broadcasted_iota(jnp.int32, full.shape, 0) // t_new)
        yc_ref[...] = jnp.sum(jnp.where(msel, full, 0.0).reshape(n_mla_heads, t_new, vw), axis=0)


def _sample_attn_call(page_table, qm, ql, qp, kn, vn, cn, rn, wuv, caches, layer, t_new, mla_scale,
                      n_sb_heads, n_mla_heads, g_pages):
    cache_k, cache_v, cache_c, cache_r = caches
    nb, n_pages = page_table.shape
    page = cache_k.shape[2]
    sbw, kvr, rope = cache_k.shape[3], cache_c.shape[3], cache_r.shape[3]
    vw = wuv.shape[1]
    steps = n_pages // g_pages
    pt_flat = page_table.reshape(-1)

    def seq_spec(a):
        return pl.BlockSpec((None,) + a.shape[1:], lambda b, c, pt: (b,) + (0,) * (a.ndim - 1))

    def page_spec(width, g):
        return pl.BlockSpec(
            (None, None, page, width),
            lambda b, c, pt: (layer, pt[b * n_pages + n_pages - 1 - (c * g_pages + g)], 0, 0))

    in_specs = [seq_spec(a) for a in (qm, ql, qp, kn, vn, cn, rn)]
    in_specs.append(pl.BlockSpec(wuv.shape, lambda b, c, pt: (0, 0)))
    args = [qm, ql, qp, kn, vn, cn, rn, wuv]
    for g in range(g_pages):
        in_specs += [page_spec(sbw, g), page_spec(sbw, g), page_spec(kvr, g), page_spec(rope, g)]
        args += [cache_k, cache_v, cache_c, cache_r]
    rows_sb, rows_mla = qm.shape[1], ql.shape[1]
    grid_spec = pltpu.PrefetchScalarGridSpec(
        num_scalar_prefetch=1,
        grid=(nb, steps),
        in_specs=in_specs,
        out_specs=[
            pl.BlockSpec((None, t_new, sbw), lambda b, c, pt: (b, 0, 0)),
            pl.BlockSpec((None, t_new, vw), lambda b, c, pt: (b, 0, 0)),
        ],
        scratch_shapes=[
            pltpu.VMEM((rows_sb, 1), F32), pltpu.VMEM((rows_sb, sbw), F32),
            pltpu.VMEM((rows_mla, 1), F32), pltpu.VMEM((rows_mla, 1), F32), pltpu.VMEM((rows_mla, kvr), F32),
        ],
    )
    return pl.pallas_call(
        functools.partial(_sample_attn_kernel, (g_pages, n_sb_heads, n_mla_heads, t_new, mla_scale)),
        out_shape=[jax.ShapeDtypeStruct((nb, t_new, sbw), F32), jax.ShapeDtypeStruct((nb, t_new, vw), F32)],
        grid_spec=grid_spec,
        compiler_params=_cparams(("arbitrary", "arbitrary")),
        name="sample_attn",
    )(pt_flat, *args)


def _conv_kernel(taps, u_ref, halo_ref, w_ref, b_ref, g_ref, beta_ref, o_ref, scr_ref):
    tt = u_ref.shape[1]
    scr_ref[:, :CONV_HALO, :] = halo_ref[...]
    scr_ref[:, CONV_HALO:, :] = u_ref[...]
    first = CONV_HALO - (taps - 1)
    y = jnp.zeros(u_ref.shape, F32)
    for j in range(taps):
        y = y + scr_ref[:, first + j:first + j + tt, :] * w_ref[j:j + 1, :]
    y = y + b_ref[...]
    mu = jnp.mean(y, axis=-1, keepdims=True)
    yc = y - mu
    var = jnp.mean(yc * yc, axis=-1, keepdims=True)
    y = yc * lax.rsqrt(var + EPS) * g_ref[...] + beta_ref[...]
    o_ref[...] = (y * _sigmoid(y)).astype(o_ref.dtype)


def _conv_call(u3, halo, w_dw, b_dw, g, beta, tb):
    nb, tt, cw = u3.shape
    taps = w_dw.shape[0]
    blk = lambda t: pl.BlockSpec((tb, t, cw), lambda i: (i, 0, 0))
    return pl.pallas_call(
        functools.partial(_conv_kernel, taps),
        out_shape=jax.ShapeDtypeStruct((nb, tt, cw), BF16),
        grid=(nb // tb,),
        in_specs=[blk(tt), blk(CONV_HALO), _const_spec((taps, cw)), _const_spec((1, cw)), _const_spec((1, cw)),
                  _const_spec((1, cw))],
        out_specs=blk(tt),
        scratch_shapes=[pltpu.VMEM((tb, CONV_HALO + tt, cw), F32)],
        compiler_params=_cparams(("arbitrary",)),
        name="conv_module",
    )(u3, halo, w_dw, b_dw, g, beta)


def _merge_kernel(x_ref, sh_ref, sc_ref, gt_ref, g1_ref, ya_ref, cact_ref, yc_ref, wg_ref, wsb_ref, wcv_ref,
                  wml_ref, wout_ref, o_ref):
    x = x_ref[...]
    d = x.shape[1]
    h = (_rms(x, g1_ref[...]) * (1.0 + sc_ref[...]) + sh_ref[...]).astype(BF16)
    branches = (
        _dot(ya_ref[...].astype(BF16), wsb_ref[...]),
        _dot(cact_ref[...].astype(BF16), wcv_ref[...]),
        _dot(yc_ref[...].astype(BF16), wml_ref[...]),
    )
    merged = jnp.zeros_like(x)
    for k, br in enumerate(branches):
        merged = merged + _sigmoid(_dot(h, wg_ref[:, k * d:(k + 1) * d])) * br
    o_ref[...] = x + gt_ref[...] * _dot(merged.astype(BF16), wout_ref[...])


def _merge_call(x, mod, g1, ya, cact, yc, lw, per_row, tm, tiles_per_seq):
    rows, d = x.shape
    row = lambda w: pl.BlockSpec((tm, w), lambda i: (i, 0))
    ws = [lw["wg"], lw["wsb"], lw["wcv"], lw["wml"], lw["wout"]]
    return pl.pallas_call(
        _merge_kernel,
        out_shape=jax.ShapeDtypeStruct((rows, d), F32),
        grid=(rows // tm,),
        in_specs=[row(d)] + [_mod_spec(per_row, tm, d, k, tiles_per_seq) for k in (0, 1, 2)]
        + [_const_spec((1, d)), row(ya.shape[1]), row(cact.shape[1]), row(yc.shape[1])]
        + [_const_spec(w.shape) for w in ws],
        out_specs=row(d),
        compiler_params=_cparams(("arbitrary",)),
        name="merge_out",
    )(x, mod, mod, mod, g1, ya, cact, yc, *ws)


def _ffn_kernel(chunk, final, x_ref, sh_ref, sc_ref, gt_ref, g2_ref, w1_ref, w2_ref, fg_ref, o_ref):
    x = x_ref[...]
    h = (_rms(x, g2_ref[...]) * (1.0 + sc_ref[...]) + sh_ref[...]).astype(BF16)
    acc = jnp.zeros_like(x)
    for c in range(w1_ref.shape[1] // chunk):
        t = jnp.maximum(_dot(h, w1_ref[:, c * chunk:(c + 1) * chunk]), 0.0)
        acc = acc + _dot((t * t).astype(BF16), w2_ref[c * chunk:(c + 1) * chunk, :])
    y = x + gt_ref[...] * acc
    if final:
        y = _rms(y, fg_ref[...])
    o_ref[...] = y


def _ffn_call(x, mod, g2, w1, w2, fg, final, per_row, tm, tiles_per_seq):
    rows, d = x.shape
    row = pl.BlockSpec((tm, d), lambda i: (i, 0))
    return pl.pallas_call(
        functools.partial(_ffn_kernel, 512, final),
        out_shape=jax.ShapeDtypeStruct((rows, d), F32),
        grid=(rows // tm,),
        in_specs=[row] + [_mod_spec(per_row, tm, d, k, tiles_per_seq) for k in (3, 4, 5)]
        + [_const_spec((1, d)), _const_spec(w1.shape), _const_spec(w2.shape), _const_spec((1, d))],
        out_specs=row,
        compiler_params=_cparams(("arbitrary",)),
        name="ffn_final" if final else "ffn",
    )(x, mod, mod, mod, g2, w1, w2, fg)


def _rot_cols(w):
    half = w.shape[-1] // 2
    return jnp.concatenate([-w[..., half:], w[..., :half]], axis=-1)


def _pad_cols(w, width):
    return jnp.pad(w, [(0, 0)] * (w.ndim - 1) + [(0, width - w.shape[-1])])


def _prep_layer(l, dims, w_in, w_o_sb, w_o_conv, w_uq, w_uk, w_uv, w_o_mla, w_out, w_ff1, w_ff2,
                q_norm_g, kv_norm_g):
    sbw, cw, qr, kvr, rope, _ = dims
    d = w_in.shape[1]
    nh, nope, vd = w_uk.shape[2], w_uk.shape[3], w_uv.shape[3]
    wi = w_in[l]
    o = 3 * sbw + 2 * cw + qr + kvr
    w_kpe = wi[:, o:o + rope]
    w1 = jnp.concatenate([wi[:, :o], _pad_cols(w_kpe, LANES), _pad_cols(_rot_cols(w_kpe), LANES)], axis=1)
    wg = wi[:, o + rope:]
    wq = w_uq[l].reshape(qr, nh, nope + rope)
    wq_nope, wq_pe = wq[..., :nope], wq[..., nope:]
    wq_rot = _rot_cols(wq_pe)
    zn = jnp.zeros_like(wq_nope)
    wqa_p = _pad_cols(jnp.concatenate([wq_nope, wq_pe], -1), LANES).reshape(qr, nh * LANES)
    wqb_p = _pad_cols(jnp.concatenate([zn, wq_rot], -1), LANES).reshape(qr, nh * LANES)
    wqa_s = jnp.concatenate([wq_nope.reshape(qr, -1), wq_pe.reshape(qr, -1)], axis=1)
    wqb_s = jnp.concatenate([zn.reshape(qr, -1), wq_rot.reshape(qr, -1)], axis=1)
    wk = _pad_cols(w_uk[l], LANES).reshape(kvr, nh * LANES)
    ek_head = jnp.concatenate([jnp.zeros((rope, nope), F32), jnp.eye(rope, dtype=F32),
                               jnp.zeros((rope, LANES - nope - rope), F32)], axis=1)
    ek = jnp.tile(ek_head, (1, nh))
    wv = w_uv[l].reshape(kvr, nh * vd)
    wabs = jnp.einsum('chd,hg->hdgc', w_uk[l], jnp.eye(nh, dtype=F32)).reshape(nh * nope, nh * kvr)
    b = lambda a: a.astype(BF16)
    return {
        "w1": b(w1), "wg": b(wg), "wqa_p": b(wqa_p), "wqb_p": b(wqb_p), "wqa_s": b(wqa_s), "wqb_s": b(wqb_s),
        "wk": b(wk), "ek": b(ek), "wv": b(wv), "wabs": b(wabs),
        "wsb": b(w_o_sb[l]), "wcv": b(w_o_conv[l]), "wml": b(w_o_mla[l]), "wout": b(w_out[l]),
        "wff1": b(w_ff1[l]), "wff2": b(w_ff2[l]),
        "qg": q_norm_g[l][None, :], "kvg": kv_norm_g[l][None, :],
    }


def _rope_tables(pos, rope, nh, nope, prompt_layout):
    half = rope // 2
    inv = ROPE_BASE ** (-jnp.arange(half, dtype=F32) / half)
    ang = pos.astype(F32)[:, None] * inv[None, :]
    cos = jnp.concatenate([jnp.cos(ang)] * 2, axis=1)
    sin = jnp.concatenate([jnp.sin(ang)] * 2, axis=1)
    n = pos.shape[0]
    if prompt_layout:
        tc = jnp.concatenate([jnp.ones((n, nope), F32), cos, jnp.zeros((n, LANES - nope - rope), F32)], axis=1)
        ts = jnp.concatenate([jnp.zeros((n, nope), F32), sin, jnp.zeros((n, LANES - nope - rope), F32)], axis=1)
        tc, ts = jnp.tile(tc, (1, nh)), jnp.tile(ts, (1, nh))
    else:
        tc = jnp.concatenate([jnp.ones((n, nh * nope), F32), jnp.tile(cos, (1, nh))], axis=1)
        ts = jnp.concatenate([jnp.zeros((n, nh * nope), F32), jnp.tile(sin, (1, nh))], axis=1)
    return tc, ts, cos, sin


def _pick_tile(n, target):
    t = min(n, target)
    while n % t:
        t //= 2
    return t


def kernel(x_prompt, x_sample, c_prompt, c_sample, cache_sb_k, cache_sb_v, cache_mla_ckv, cache_mla_kpe,
           state_conv, page_table, norm1_g, norm2_g, final_g, w_ada, b_ada, w_in, w_o_sb, w_dw, b_dw, cln_g,
           cln_b, w_o_conv, q_norm_g, w_uq, kv_norm_g, w_uk, w_uv, w_o_mla, w_out, w_ff1, w_ff2):
    bsz, seq, d = x_prompt.shape
    nb, t_new, _ = x_sample.shape
    depth, n_pool, page, n_sb_heads, sb_dh = cache_sb_k.shape
    sbw = n_sb_heads * sb_dh
    cw, taps = w_dw.shape[2], w_dw.shape[1]
    qr, kvr, rope = w_uq.shape[1], w_uk.shape[1], cache_mla_kpe.shape[-1]
    nh, nope, vd = w_uk.shape[2], w_uk.shape[3], w_uv.shape[3]
    n_pages = page_table.shape[1]
    past_len = n_pages * page
    mla_scale = float((nope + rope) ** -0.5)
    dims = (sbw, cw, qr, kvr, rope, float(sb_dh ** -0.5))
    assert taps - 1 <= CONV_HALO and sbw % LANES == 0 and nope + rope <= LANES

    rows_p, rows_s = bsz * seq, nb * t_new
    tm_p = _pick_tile(seq, 512)
    tm_s = _pick_tile(rows_s, 256)
    tq = _pick_tile(seq, 256)
    tiles_per_seq = seq // tm_p

    n_c = bsz + nb
    c_all = jnp.pad(jnp.concatenate([c_prompt, c_sample], axis=0), ((0, -n_c % 8), (0, 0)))
    mod = _mod_call(c_all, w_ada.astype(BF16), b_ada)
    mod_p = mod[:, :bsz, None, :]
    mod_s = jnp.repeat(mod[:, bsz:n_c], t_new, axis=1)

    pos_p = jnp.arange(seq, dtype=jnp.int32)
    pos_s = past_len + jnp.arange(t_new, dtype=jnp.int32)
    tabs_p = _rope_tables(pos_p, rope, nh, nope, True)
    tabs_s = tuple(jnp.tile(t, (nb, 1)) for t in _rope_tables(pos_s, rope, nh, nope, False))

    caches = (cache_sb_k.reshape(depth, n_pool, page, sbw), cache_sb_v.reshape(depth, n_pool, page, sbw),
              cache_mla_ckv, cache_mla_kpe)
    head_of_col = jnp.arange(sbw, dtype=jnp.int32) // sb_dh
    head_mask = (head_of_col[None, :] == jnp.arange(n_sb_heads, dtype=jnp.int32)[:, None])

    xp = x_prompt.reshape(rows_p, d)
    xs = x_sample.reshape(rows_s, d)
    rows_out_p, rows_out_s = [], []
    for l in range(depth):
        lw = _prep_layer(l, dims, w_in, w_o_sb, w_o_conv, w_uq, w_uk, w_uv, w_o_mla, w_out, w_ff1, w_ff2,
                         q_norm_g, kv_norm_g)
        g1, g2 = norm1_g[l][None, :], norm2_g[l][None, :]
        conv_w = (w_dw[l], b_dw[l][None, :], cln_g[l][None, :], cln_b[l][None, :])
        final = l == depth - 1
        fg = final_g[None, :]

        lw_p = dict(lw, wqa=lw["wqa_p"], wqb=lw["wqb_p"])
        (qab, kab, vab, ka, va, u, q, kf, vc, ckv, kpe) = _in_proj_call(
            xp, mod_p[l], g1, lw_p, tabs_p, dims, True, tm_p, tiles_per_seq)
        r3 = lambda a: a.reshape(bsz, seq, a.shape[-1])
        ya = _sb_prompt_call(r3(qab), r3(kab), r3(vab), tq).reshape(rows_p, sbw)
        yc = _mla_prompt_call(r3(q), r3(kf), r3(vc), tq, mla_scale).reshape(rows_p, nh * vd)
        u4 = u.reshape(bsz, tiles_per_seq, tm_p, cw)
        halo = jnp.concatenate([jnp.zeros((bsz, 1, CONV_HALO, cw), F32), u4[:, :-1, tm_p - CONV_HALO:, :]], axis=1)
        cact = _conv_call(u4.reshape(bsz * tiles_per_seq, tm_p, cw), halo.reshape(bsz * tiles_per_seq, CONV_HALO, cw),
                          *conv_w, 1).reshape(rows_p, cw)
        xp = _merge_call(xp, mod_p[l], g1, ya, cact, yc, lw, False, tm_p, tiles_per_seq)
        xp = _ffn_call(xp, mod_p[l], g2, lw["wff1"], lw["wff2"], fg, final, False, tm_p, tiles_per_seq)
        rows_out_p.append((ka.reshape(bsz, seq, n_sb_heads, sb_dh), va.reshape(bsz, seq, n_sb_heads, sb_dh),
                           ckv.reshape(bsz, seq, kvr), kpe.reshape(bsz, seq, rope),
                           u.reshape(bsz, seq, cw)[:, seq - (taps - 1):, :]))

        lw_s = dict(lw, wqa=lw["wqa_s"], wqb=lw["wqb_s"])
        (qab, ka, va, u, qlat, qpe, ckv, kpe) = _in_proj_call(
            xs, mod_s[l], g1, lw_s, tabs_s, dims, False, tm_s, 1)
        qm = jnp.where(head_mask[None, :, None, :], qab.reshape(nb, 1, t_new, sbw), 0).reshape(
            nb, n_sb_heads * t_new, sbw).astype(BF16)
        ql = qlat.reshape(nb, t_new, nh, kvr).transpose(0, 2, 1, 3).reshape(nb, nh * t_new, kvr)
        qp = qpe.reshape(nb, t_new, nh, rope).transpose(0, 2, 1, 3).reshape(nb, nh * t_new, rope)
        padk = lambda a: jnp.pad(a.reshape(nb, t_new, a.shape[-1]).astype(BF16), ((0, 0), (0, page - t_new), (0, 0)))
        ya, yc = _sample_attn_call(page_table, qm, ql, qp, padk(ka), padk(va), padk(ckv), padk(kpe), lw["wv"],
                                   caches, l, t_new, mla_scale, n_sb_heads, nh, _pick_tile(n_pages, 8))
        u3 = u.reshape(nb, t_new, cw)
        state = state_conv[l]
        halo = jnp.pad(state, ((0, 0), (CONV_HALO - (taps - 1), 0), (0, 0)))
        cact = _conv_call(u3, halo, *conv_w, _pick_tile(nb, 32)).reshape(rows_s, cw)
        xs = _merge_call(xs, mod_s[l], g1, ya.reshape(rows_s, sbw), cact, yc.reshape(rows_s, nh * vd), lw, True,
                         tm_s, 1)
        xs = _ffn_call(xs, mod_s[l], g2, lw["wff1"], lw["wff2"], fg, final, True, tm_s, 1)
        new_state = jnp.concatenate([state, u3], axis=1)[:, -(taps - 1):, :]
        rows_out_s.append((ka.reshape(nb, t_new, n_sb_heads, sb_dh), va.reshape(nb, t_new, n_sb_heads, sb_dh),
                           ckv.reshape(nb, t_new, kvr), kpe.reshape(nb, t_new, rope), new_state))

    stk = lambda rows, i: jnp.stack([r[i] for r in rows], axis=0)
    return (xp.reshape(bsz, seq, d), xs.reshape(nb, t_new, d),
            stk(rows_out_p, 0), stk(rows_out_p, 1), stk(rows_out_p, 2), stk(rows_out_p, 3), stk(rows_out_p, 4),
            stk(rows_out_s, 0), stk(rows_out_s, 1), stk(rows_out_s, 2), stk(rows_out_s, 3), stk(rows_out_s, 4))
```

```python
import functools

import jax
import jax.numpy as jnp
from jax import lax
from jax.experimental import pallas as pl
from jax.experimental.pallas import tpu as pltpu

F32 = jnp.float32
BF16 = jnp.bfloat16

EPS = 1e-6
ROPE_BASE = 10000.0
LANES = 128
MXU_DIM = 256
LOG2E = 1.4426950408889634
CONV_HALO = 32
VMEM_LIMIT = 56 * 1024 * 1024
ROW_TILE = 512
ATTN_TILE = 512
PAGES_PER_STEP = 16


def _cparams(sem):
    return pltpu.CompilerParams(dimension_semantics=sem, vmem_limit_bytes=VMEM_LIMIT)


def _dot(a, b):
    return jnp.dot(a, b, preferred_element_type=F32)


def _dot_nt(a, b):
    return lax.dot_general(a, b, (((1,), (1,)), ((), ())), preferred_element_type=F32)


def _rms(x, g):
    return x * lax.rsqrt(jnp.mean(x * x, axis=-1, keepdims=True) + EPS) * g


def _sigmoid(x):
    return 1.0 / (1.0 + jnp.exp(-x))


def _neg_softplus(z):
    return -(jnp.maximum(z, 0.0) + jnp.log(1.0 + jnp.exp(-jnp.abs(z))))


def _suffix_matrix(n):
    r = lax.broadcasted_iota(jnp.int32, (n, n), 0)
    c = lax.broadcasted_iota(jnp.int32, (n, n), 1)
    return jnp.where(r > c, 1.0, 0.0).astype(BF16)


def _suffix_sums(ls, u, stack):
    rows = ls[0].shape[0]
    his = [l.astype(BF16) for l in ls]
    los = [(l - hi.astype(F32)).astype(BF16) for l, hi in zip(ls, his)]
    if not stack:
        return [_dot(hi, u) + _dot(lo, u) for hi, lo in zip(his, los)]
    s = _dot(jnp.concatenate(his + los, axis=0), u)
    n = len(ls)
    return [s[g * rows:(g + 1) * rows] + s[(n + g) * rows:(n + g + 1) * rows] for g in range(n)]


def _mod_kernel(c_ref, w_ref, b_ref, o_ref):
    c = c_ref[...]
    a = (c * _sigmoid(c)).astype(BF16)
    o_ref[...] = _dot(a, w_ref[...]) + b_ref[...]


def _mod_call(c_all, w_ada, b_ada):
    depth, d, n = w_ada.shape
    rows = c_all.shape[0]
    tn = 1024
    return pl.pallas_call(
        _mod_kernel,
        out_shape=jax.ShapeDtypeStruct((depth, rows, n), F32),
        grid=(depth, n // tn),
        in_specs=[
            pl.BlockSpec((rows, d), lambda l, j: (0, 0)),
            pl.BlockSpec((None, d, tn), lambda l, j: (l, 0, j)),
            pl.BlockSpec((None, 1, tn), lambda l, j: (l, 0, j)),
        ],
        out_specs=pl.BlockSpec((None, rows, tn), lambda l, j: (l, 0, j)),
        compiler_params=_cparams(("arbitrary", "arbitrary")),
        name="ada_mod",
    )(c_all, w_ada, b_ada.reshape(depth, 1, n))


def _mod_spec(per_row, tm, d, k, tiles_per_seq):
    if per_row:
        return pl.BlockSpec((tm, d), lambda i: (i, k))
    return pl.BlockSpec((None, 1, d), lambda i: (i // tiles_per_seq, 0, k))


def _const_spec(shape):
    nd = len(shape)
    return pl.BlockSpec(shape, lambda i: (0,) * nd)


def _in_proj_kernel(dims, prompt, *refs):
    sbw, cw, qr, kvr, rope, sb_scale = dims
    if prompt:
        (x_ref, sh_ref, sc_ref, g1_ref, w1_ref, qg_ref, wqa_ref, wqb_ref, kvg_ref, tc_ref, ts_ref, kc_ref,
         ks_ref, wk_ref, ek_ref, wv_ref,
         qab_ref, kab_ref, vab_ref, ka_ref, va_ref, u_ref, q_ref, kf_ref, vc_ref, ckv_ref, kpe_ref) = refs
    else:
        (x_ref, sh_ref, sc_ref, g1_ref, w1_ref, qg_ref, wqa_ref, wqb_ref, kvg_ref, tc_ref, ts_ref, kc_ref,
         ks_ref, wabs_ref,
         qab_ref, ka_ref, va_ref, u_ref, qlat_ref, qpe_ref, ckv_ref, kpe_ref) = refs

    x = x_ref[...]
    h = _rms(x, g1_ref[...]) * (1.0 + sc_ref[...]) + sh_ref[...]
    p = _dot(h.astype(BF16), w1_ref[...])
    o = 0
    qa = p[:, o:o + sbw]; o += sbw
    ka = p[:, o:o + sbw]; o += sbw
    va = p[:, o:o + sbw]; o += sbw
    ga = p[:, o:o + cw]; o += cw
    gb = p[:, o:o + cw]; o += cw
    cq = p[:, o:o + qr]; o += qr
    ckv = p[:, o:o + kvr]; o += kvr
    kpe = p[:, o:o + LANES][:, :rope]; o += LANES
    kpe_rot = p[:, o:o + LANES][:, :rope]

    qab_ref[...] = (qa * sb_scale).astype(BF16)
    ka_ref[...] = ka
    va_ref[...] = va
    u_ref[...] = ga * _sigmoid(gb)

    cqn = _rms(cq, qg_ref[...]).astype(BF16)
    q = _dot(cqn, wqa_ref[...]) * tc_ref[...] + _dot(cqn, wqb_ref[...]) * ts_ref[...]
    ckvn = _rms(ckv, kvg_ref[...])
    kper = kpe * kc_ref[...] + kpe_rot * ks_ref[...]
    ckv_ref[...] = ckvn
    kpe_ref[...] = kper
    if prompt:
        kab_ref[...] = ka.astype(BF16)
        vab_ref[...] = va.astype(BF16)
        q_ref[...] = q.astype(BF16)
        cb = ckvn.astype(BF16)
        kf_ref[...] = (_dot(cb, wk_ref[...]) + _dot(kper.astype(BF16), ek_ref[...])).astype(BF16)
        vc_ref[...] = _dot(cb, wv_ref[...]).astype(BF16)
    else:
        nope_w = wabs_ref.shape[0]
        qlat_ref[...] = _dot(q[:, :nope_w].astype(BF16), wabs_ref[...]).astype(BF16)
        qpe_ref[...] = q[:, nope_w:].astype(BF16)


def _in_proj_call(x, mod, g1, lw, tabs, dims, prompt, tm, tiles_per_seq):
    rows, d = x.shape
    sbw, cw, qr, kvr, rope, _ = dims
    tc, ts, kc, ks = tabs
    ntab = tc.shape[0] // tm
    wq = tc.shape[1]
    per_row = not prompt
    row = lambda w: pl.BlockSpec((tm, w), lambda i: (i, 0))
    tab = lambda w: pl.BlockSpec((tm, w), lambda i: (i % ntab, 0))
    in_specs = [
        row(d),
        _mod_spec(per_row, tm, d, 0, tiles_per_seq),
        _mod_spec(per_row, tm, d, 1, tiles_per_seq),
        _const_spec((1, d)),
        _const_spec(lw["w1"].shape),
        _const_spec((1, qr)),
        _const_spec(lw["wqa"].shape),
        _const_spec(lw["wqb"].shape),
        _const_spec((1, kvr)),
        tab(wq), tab(wq), tab(rope), tab(rope),
    ]
    args = [x, mod, mod, g1, lw["w1"], lw["qg"], lw["wqa"], lw["wqb"], lw["kvg"], tc, ts, kc, ks]
    sds = jax.ShapeDtypeStruct
    if prompt:
        in_specs += [_const_spec(lw["wk"].shape), _const_spec(lw["ek"].shape), _const_spec(lw["wv"].shape)]
        args += [lw["wk"], lw["ek"], lw["wv"]]
        kfw, vcw = lw["wk"].shape[1], lw["wv"].shape[1]
        out_shape = [sds((rows, sbw), BF16)] * 3 + [sds((rows, sbw), F32)] * 2 + [
            sds((rows, cw), F32), sds((rows, wq), BF16), sds((rows, kfw), BF16), sds((rows, vcw), BF16),
            sds((rows, kvr), F32), sds((rows, rope), F32)]
        out_specs = [row(sbw)] * 5 + [row(cw), row(wq), row(kfw), row(vcw), row(kvr), row(rope)]
    else:
        in_specs += [_const_spec(lw["wabs"].shape)]
        args += [lw["wabs"]]
        nope_w, latw = lw["wabs"].shape
        out_shape = [sds((rows, sbw), BF16)] + [sds((rows, sbw), F32)] * 2 + [
            sds((rows, cw), F32), sds((rows, latw), BF16), sds((rows, wq - nope_w), BF16),
            sds((rows, kvr), F32), sds((rows, rope), F32)]
        out_specs = [row(sbw)] * 3 + [row(cw), row(latw), row(wq - nope_w), row(kvr), row(rope)]
    return pl.pallas_call(
        functools.partial(_in_proj_kernel, dims, prompt),
        out_shape=out_shape,
        grid=(rows // tm,),
        in_specs=in_specs,
        out_specs=out_specs,
        compiler_params=_cparams(("arbitrary",)),
        name="in_proj_prompt" if prompt else "in_proj_sample",
    )(*args)


def _sb_weights(z, l, carry, u, sub):
    ngroups = z.shape[1] // sub
    lgs = [l[:, g * sub:(g + 1) * sub] for g in range(ngroups)]
    es = _suffix_sums(lgs, u, False)
    for g in reversed(range(ngroups)):
        es[g] = es[g] + carry
        carry = es[g][:, :1] + lgs[g][:, :1]
    e = es[0] if ngroups == 1 else jnp.concatenate(es, axis=1)
    return jnp.exp(z + l + e), carry


def _sb_prompt_kernel(tq, sub, q_ref, k_ref, v_ref, o_ref):
    i = pl.program_id(2)
    half = q_ref.shape[1] // 2
    q = q_ref[...]
    lane = lax.broadcasted_iota(jnp.int32, (1, q.shape[1]), 1)
    first = lane < half
    zero = jnp.zeros_like(q)
    qh = (jnp.where(first, q, zero), jnp.where(first, zero, q))
    u = _suffix_matrix(sub)

    def block(j, state, masked):
        kb = k_ref[pl.ds(pl.multiple_of(j * tq, tq), tq), :]
        vb = v_ref[pl.ds(pl.multiple_of(j * tq, tq), tq), :]
        if masked:
            causal = (lax.broadcasted_iota(jnp.int32, (tq, tq), 1) < lax.broadcasted_iota(jnp.int32, (tq, tq), 0))
        new = []
        for h in range(2):
            carry, acc = state[h]
            z = _dot_nt(qh[h], kb)
            l = _neg_softplus(z)
            if masked:
                l = jnp.where(causal, l, 0.0)
            w, carry = _sb_weights(z, l, carry, u, sub)
            if masked:
                w = jnp.where(causal, w, 0.0)
            acc = acc + _dot(w.astype(BF16), vb)
            new.append((carry, acc))
        return tuple(new)

    init = tuple((jnp.zeros((tq, 1), F32), jnp.zeros((tq, q.shape[1]), F32)) for _ in range(2))
    state = block(i, init, True)
    state = lax.fori_loop(0, i, lambda t, s: block(i - 1 - t, s, False), state)
    o_ref[...] = jnp.where(first, state[0][1], state[1][1]).astype(o_ref.dtype)


def _sb_prompt_call(q, k, v, tq):
    b, s, w = q.shape
    pw = LANES
    npair = w // pw
    return pl.pallas_call(
        functools.partial(_sb_prompt_kernel, tq, min(tq, MXU_DIM)),
        out_shape=jax.ShapeDtypeStruct((b, s, w), BF16),
        grid=(b, npair, s // tq),
        in_specs=[
            pl.BlockSpec((None, tq, pw), lambda bi, hp, i: (bi, i, hp)),
            pl.BlockSpec((None, s, pw), lambda bi, hp, i: (bi, 0, hp)),
            pl.BlockSpec((None, s, pw), lambda bi, hp, i: (bi, 0, hp)),
        ],
        out_specs=pl.BlockSpec((None, tq, pw), lambda bi, hp, i: (bi, i, hp)),
        compiler_params=_cparams(("arbitrary", "arbitrary", "arbitrary")),
        name="sb_prompt",
    )(q, k, v)


def _mla_prompt_kernel(tq, scale, q_ref, k_ref, v_ref, o_ref):
    i = pl.program_id(2)
    vw = v_ref.shape[1]
    lane = lax.broadcasted_iota(jnp.int32, (1, vw), 1)
    first = lane < vw // 2
    qh = (q_ref[:, :LANES], q_ref[:, LANES:])
    scale2 = scale * LOG2E

    def block(j, state, masked):
        start = pl.multiple_of(j * tq, tq)
        vb = v_ref[pl.ds(start, tq), :]
        if masked:
            causal = (lax.broadcasted_iota(jnp.int32, (tq, tq), 1) <= lax.broadcasted_iota(jnp.int32, (tq, tq), 0))
        new = []
        for h in range(2):
            m, l, acc = state[h]
            kb = k_ref[pl.ds(start, tq), h * LANES:(h + 1) * LANES]
            s = _dot_nt(qh[h], kb) * scale2
            if masked:
                s = jnp.where(causal, s, -jnp.inf)
            m_new = jnp.maximum(m, jnp.max(s, axis=-1, keepdims=True))
            a = jnp.exp2(m - m_new)
            p = jnp.exp2(s - m_new)
            l = l * a + jnp.sum(p, axis=-1, keepdims=True)
            acc = acc * a + _dot(p.astype(BF16), vb)
            new.append((m_new, l, acc))
        return tuple(new)

    init = tuple((jnp.full((tq, 1), -jnp.inf, F32), jnp.zeros((tq, 1), F32), jnp.zeros((tq, vw), F32))
                 for _ in range(2))
    state = block(i, init, True)
    state = lax.fori_loop(0, i, lambda t, s: block(i - 1 - t, s, False), state)
    out = jnp.where(first, state[0][2] / state[0][1], state[1][2] / state[1][1])
    o_ref[...] = out.astype(o_ref.dtype)


def _mla_prompt_call(q, k, v, tq, scale):
    b, s, qw = q.shape
    vw = v.shape[2]
    npair = qw // (2 * LANES)
    pv = vw // npair
    return pl.pallas_call(
        functools.partial(_mla_prompt_kernel, tq, scale),
        out_shape=jax.ShapeDtypeStruct((b, s, vw), BF16),
        grid=(b, npair, s // tq),
        in_specs=[
            pl.BlockSpec((None, tq, 2 * LANES), lambda bi, hp, i: (bi, i, hp)),
            pl.BlockSpec((None, s, 2 * LANES), lambda bi, hp, i: (bi, 0, hp)),
            pl.BlockSpec((None, s, pv), lambda bi, hp, i: (bi, 0, hp)),
        ],
        out_specs=pl.BlockSpec((None, tq, pv), lambda bi, hp, i: (bi, i, hp)),
        compiler_params=_cparams(("arbitrary", "arbitrary", "arbitrary")),
        name="mla_prompt",
    )(q, k, v)


def _sample_attn_kernel(cfg, pt_ref, *refs):
    g_pages, n_sb_heads, n_mla_heads, t_new, mla_scale = cfg
    (qm_ref, ql_ref, qp_ref, kn_ref, vn_ref, cn_ref, rn_ref, wuv_ref) = refs[:8]
    pages = refs[8:8 + 4 * g_pages]
    ya_ref, yc_ref = refs[8 + 4 * g_pages:10 + 4 * g_pages]
    carry_ref, acc_ref, m_ref, l_ref, lat_ref = refs[10 + 4 * g_pages:]
    c = pl.program_id(1)
    page = kn_ref.shape[1]
    u = _suffix_matrix(page)
    qm = qm_ref[...]
    ql = ql_ref[...]
    qp = qp_ref[...]
    scale2 = mla_scale * LOG2E

    def token_masks(shape):
        qi = lax.broadcasted_iota(jnp.int32, shape, 0) % t_new
        ki = lax.broadcasted_iota(jnp.int32, shape, 1)
        return ki < qi, ki <= qi

    def sb_chunk(kts, vts, masked):
        zs = [_dot(qm, kt) for kt in kts]
        ls = [_neg_softplus(z) for z in zs]
        if masked:
            valid = token_masks(zs[0].shape)[0]
            ls = [jnp.where(valid, l, 0.0) for l in ls]
        sfx = _suffix_sums(ls, u, True)
        carry = carry_ref[...]
        acc = acc_ref[...]
        for z, l, e, vt in zip(zs, ls, sfx, vts):
            w = jnp.exp(z + l + (e + carry))
            if masked:
                w = jnp.where(valid, w, 0.0)
            carry = carry + (e[:, :1] + l[:, :1])
            acc = acc + _dot_nt(w.astype(BF16), vt)
        carry_ref[...] = carry
        acc_ref[...] = acc

    def mla_chunk(cbs, rts, masked):
        ss = [(_dot_nt(ql, cb) + _dot(qp, rt)) * scale2 for cb, rt in zip(cbs, rts)]
        if masked:
            valid = token_masks(ss[0].shape)[1]
            ss = [jnp.where(valid, s, -jnp.inf) for s in ss]
        m = m_ref[...]
        m_new = jnp.maximum(m, jnp.max(functools.reduce(jnp.maximum, ss), axis=-1, keepdims=True))
        a = jnp.exp2(m - m_new)
        ps = [jnp.exp2(s - m_new) for s in ss]
        l_ref[...] = l_ref[...] * a + jnp.sum(functools.reduce(jnp.add, ps), axis=-1, keepdims=True)
        lat = lat_ref[...] * a
        for p, cb in zip(ps, cbs):
            lat = lat + _dot(p.astype(BF16), cb)
        lat_ref[...] = lat
        m_ref[...] = m_new

    @pl.when(c == 0)
    def _():
        carry_ref[...] = jnp.zeros_like(carry_ref)
        acc_ref[...] = jnp.zeros_like(acc_ref)
        m_ref[...] = jnp.full_like(m_ref, -jnp.inf)
        l_ref[...] = jnp.zeros_like(l_ref)
        lat_ref[...] = jnp.zeros_like(lat_ref)
        sb_chunk([kn_ref[...]], [vn_ref[...]], True)
        mla_chunk([cn_ref[...]], [rn_ref[...]], True)

    sb_chunk([pages[4 * g][...].astype(BF16) for g in range(g_pages)],
             [pages[4 * g + 1][...].astype(BF16) for g in range(g_pages)], False)
    mla_chunk([pages[4 * g + 2][...].astype(BF16) for g in range(g_pages)],
              [pages[4 * g + 3][...].astype(BF16) for g in range(g_pages)], False)

    @pl.when(c == pl.num_programs(1) - 1)
    def _():
        acc = acc_ref[...]
        w = acc.shape[1]
        hsel = (lax.broadcasted_iota(jnp.int32, acc.shape, 1) // (w // n_sb_heads)
                == lax.broadcasted_iota(jnp.int32, acc.shape, 0) // t_new)
        ya_ref[...] = jnp.sum(jnp.where(hsel, acc, 0.0).reshape(n_sb_heads, t_new, w), axis=0)
        lat = (lat_ref[...] / l_ref[...]).astype(BF16)
        full = _dot(lat, wuv_ref[...])
        vw = full.shape[1]
        msel = (lax.broadcasted_iota(jnp.int32, full.shape, 1) // (vw // n_mla_heads)
                == lax.broadcasted_iota(jnp.int32, full.shape, 0) // t_new)
        yc_ref[...] = jnp.sum(jnp.where(msel, full, 0.0).reshape(n_mla_heads, t_new, vw), axis=0)


def _sample_attn_call(page_table, qm, ql, qp, kn, vn, cn, rn, wuv, caches, layer, t_new, mla_scale,
                      n_sb_heads, n_mla_heads, g_pages):
    nb, n_pages = page_table.shape
    sbw, kvr = caches[0].shape[2], caches[2].shape[3]
    vw = wuv.shape[1]
    steps = n_pages // g_pages
    pt_flat = page_table.reshape(-1)

    def seq_spec(a):
        return pl.BlockSpec((None,) + a.shape[1:], lambda b, c, pt: (b,) + (0,) * (a.ndim - 1))

    def page_spec(cache, g):
        return pl.BlockSpec(
            (None, None) + cache.shape[2:],
            lambda b, c, pt: (layer, pt[b * n_pages + n_pages - 1 - (c * g_pages + g)], 0, 0))

    in_specs = [seq_spec(a) for a in (qm, ql, qp, kn, vn, cn, rn)]
    in_specs.append(pl.BlockSpec(wuv.shape, lambda b, c, pt: (0, 0)))
    args = [qm, ql, qp, kn, vn, cn, rn, wuv]
    for g in range(g_pages):
        in_specs += [page_spec(cache, g) for cache in caches]
        args += list(caches)
    rows_sb, rows_mla = qm.shape[1], ql.shape[1]
    grid_spec = pltpu.PrefetchScalarGridSpec(
        num_scalar_prefetch=1,
        grid=(nb, steps),
        in_specs=in_specs,
        out_specs=[
            pl.BlockSpec((None, t_new, sbw), lambda b, c, pt: (b, 0, 0)),
            pl.BlockSpec((None, t_new, vw), lambda b, c, pt: (b, 0, 0)),
        ],
        scratch_shapes=[
            pltpu.VMEM((rows_sb, 1), F32), pltpu.VMEM((rows_sb, sbw), F32),
            pltpu.VMEM((rows_mla, 1), F32), pltpu.VMEM((rows_mla, 1), F32), pltpu.VMEM((rows_mla, kvr), F32),
        ],
    )
    return pl.pallas_call(
        functools.partial(_sample_attn_kernel, (g_pages, n_sb_heads, n_mla_heads, t_new, mla_scale)),
        out_shape=[jax.ShapeDtypeStruct((nb, t_new, sbw), F32), jax.ShapeDtypeStruct((nb, t_new, vw), F32)],
        grid_spec=grid_spec,
        compiler_params=_cparams(("arbitrary", "arbitrary")),
        name="sample_attn",
    )(pt_flat, *args)


def _conv_kernel(taps, u_ref, halo_ref, w_ref, b_ref, g_ref, beta_ref, o_ref, scr_ref):
    tt = u_ref.shape[1]
    scr_ref[:, :CONV_HALO, :] = halo_ref[...]
    scr_ref[:, CONV_HALO:, :] = u_ref[...]
    first = CONV_HALO - (taps - 1)
    y = jnp.zeros(u_ref.shape, F32)
    for j in range(taps):
        y = y + scr_ref[:, first + j:first + j + tt, :] * w_ref[j:j + 1, :]
    y = y + b_ref[...]
    mu = jnp.mean(y, axis=-1, keepdims=True)
    yc = y - mu
    var = jnp.mean(yc * yc, axis=-1, keepdims=True)
    y = yc * lax.rsqrt(var + EPS) * g_ref[...] + beta_ref[...]
    o_ref[...] = (y * _sigmoid(y)).astype(o_ref.dtype)


def _conv_call(u3, halo, w_dw, b_dw, g, beta, tb):
    nb, tt, cw = u3.shape
    taps = w_dw.shape[0]
    blk = lambda t: pl.BlockSpec((tb, t, cw), lambda i: (i, 0, 0))
    return pl.pallas_call(
        functools.partial(_conv_kernel, taps),
        out_shape=jax.ShapeDtypeStruct((nb, tt, cw), BF16),
        grid=(nb // tb,),
        in_specs=[blk(tt), blk(CONV_HALO), _const_spec((taps, cw)), _const_spec((1, cw)), _const_spec((1, cw)),
                  _const_spec((1, cw))],
        out_specs=blk(tt),
        scratch_shapes=[pltpu.VMEM((tb, CONV_HALO + tt, cw), F32)],
        compiler_params=_cparams(("arbitrary",)),
        name="conv_module",
    )(u3, halo, w_dw, b_dw, g, beta)


def _merge_kernel(x_ref, sh_ref, sc_ref, gt_ref, g1_ref, ya_ref, cact_ref, yc_ref, wg_ref, wsb_ref, wcv_ref,
                  wml_ref, wout_ref, o_ref):
    x = x_ref[...]
    d = x.shape[1]
    h = (_rms(x, g1_ref[...]) * (1.0 + sc_ref[...]) + sh_ref[...]).astype(BF16)
    branches = (
        _dot(ya_ref[...].astype(BF16), wsb_ref[...]),
        _dot(cact_ref[...].astype(BF16), wcv_ref[...]),
        _dot(yc_ref[...].astype(BF16), wml_ref[...]),
    )
    merged = jnp.zeros_like(x)
    for k, br in enumerate(branches):
        merged = merged + _sigmoid(_dot(h, wg_ref[:, k * d:(k + 1) * d])) * br
    o_ref[...] = x + gt_ref[...] * _dot(merged.astype(BF16), wout_ref[...])


def _merge_call(x, mod, g1, ya, cact, yc, lw, per_row, tm, tiles_per_seq):
    rows, d = x.shape
    row = lambda w: pl.BlockSpec((tm, w), lambda i: (i, 0))
    ws = [lw["wg"], lw["wsb"], lw["wcv"], lw["wml"], lw["wout"]]
    return pl.pallas_call(
        _merge_kernel,
        out_shape=jax.ShapeDtypeStruct((rows, d), F32),
        grid=(rows // tm,),
        in_specs=[row(d)] + [_mod_spec(per_row, tm, d, k, tiles_per_seq) for k in (0, 1, 2)]
        + [_const_spec((1, d)), row(ya.shape[1]), row(cact.shape[1]), row(yc.shape[1])]
        + [_const_spec(w.shape) for w in ws],
        out_specs=row(d),
        compiler_params=_cparams(("arbitrary",)),
        name="merge_out",
    )(x, mod, mod, mod, g1, ya, cact, yc, *ws)


def _ffn_kernel(chunk, final, x_ref, sh_ref, sc_ref, gt_ref, g2_ref, w1_ref, w2_ref, fg_ref, o_ref):
    x = x_ref[...]
    h = (_rms(x, g2_ref[...]) * (1.0 + sc_ref[...]) + sh_ref[...]).astype(BF16)
    acc = jnp.zeros_like(x)
    for c in range(w1_ref.shape[1] // chunk):
        t = jnp.maximum(_dot(h, w1_ref[:, c * chunk:(c + 1) * chunk]), 0.0)
        acc = acc + _dot((t * t).astype(BF16), w2_ref[c * chunk:(c + 1) * chunk, :])
    y = x + gt_ref[...] * acc
    if final:
        y = _rms(y, fg_ref[...])
    o_ref[...] = y


def _ffn_call(x, mod, g2, w1, w2, fg, final, per_row, tm, tiles_per_seq):
    rows, d = x.shape
    row = pl.BlockSpec((tm, d), lambda i: (i, 0))
    return pl.pallas_call(
        functools.partial(_ffn_kernel, 512, final),
        out_shape=jax.ShapeDtypeStruct((rows, d), F32),
        grid=(rows // tm,),
        in_specs=[row] + [_mod_spec(per_row, tm, d, k, tiles_per_seq) for k in (3, 4, 5)]
        + [_const_spec((1, d)), _const_spec(w1.shape), _const_spec(w2.shape), _const_spec((1, d))],
        out_specs=row,
        compiler_params=_cparams(("arbitrary",)),
        name="ffn_final" if final else "ffn",
    )(x, mod, mod, mod, g2, w1, w2, fg)


def _rot_cols(w):
    half = w.shape[-1] // 2
    return jnp.concatenate([-w[..., half:], w[..., :half]], axis=-1)


def _pad_cols(w, width):
    return jnp.pad(w, [(0, 0)] * (w.ndim - 1) + [(0, width - w.shape[-1])])


def _prep_layer(l, dims, w_in, w_o_sb, w_o_conv, w_uq, w_uk, w_uv, w_o_mla, w_out, w_ff1, w_ff2,
                q_norm_g, kv_norm_g):
    sbw, cw, qr, kvr, rope, _ = dims
    d = w_in.shape[1]
    nh, nope, vd = w_uk.shape[2], w_uk.shape[3], w_uv.shape[3]
    wi = w_in[l]
    o = 3 * sbw + 2 * cw + qr + kvr
    w_kpe = wi[:, o:o + rope]
    w1 = jnp.concatenate([wi[:, :o], _pad_cols(w_kpe, LANES), _pad_cols(_rot_cols(w_kpe), LANES)], axis=1)
    wg = wi[:, o + rope:]
    wq = w_uq[l].reshape(qr, nh, nope + rope)
    wq_nope, wq_pe = wq[..., :nope], wq[..., nope:]
    wq_rot = _rot_cols(wq_pe)
    zn = jnp.zeros_like(wq_nope)
    wqa_p = _pad_cols(jnp.concatenate([wq_nope, wq_pe], -1), LANES).reshape(qr, nh * LANES)
    wqb_p = _pad_cols(jnp.concatenate([zn, wq_rot], -1), LANES).reshape(qr, nh * LANES)
    wqa_s = jnp.concatenate([wq_nope.reshape(qr, -1), wq_pe.reshape(qr, -1)], axis=1)
    wqb_s = jnp.concatenate([zn.reshape(qr, -1), wq_rot.reshape(qr, -1)], axis=1)
    wk = _pad_cols(w_uk[l], LANES).reshape(kvr, nh * LANES)
    ek_head = jnp.concatenate([jnp.zeros((rope, nope), F32), jnp.eye(rope, dtype=F32),
                               jnp.zeros((rope, LANES - nope - rope), F32)], axis=1)
    ek = jnp.tile(ek_head, (1, nh))
    wv = w_uv[l].reshape(kvr, nh * vd)
    wabs = jnp.einsum('chd,hg->hdgc', w_uk[l], jnp.eye(nh, dtype=F32)).reshape(nh * nope, nh * kvr)
    b = lambda a: a.astype(BF16)
    return {
        "w1": b(w1), "wg": b(wg), "wqa_p": b(wqa_p), "wqb_p": b(wqb_p), "wqa_s": b(wqa_s), "wqb_s": b(wqb_s),
        "wk": b(wk), "ek": b(ek), "wv": b(wv), "wabs": b(wabs),
        "wsb": b(w_o_sb[l]), "wcv": b(w_o_conv[l]), "wml": b(w_o_mla[l]), "wout": b(w_out[l]),
        "wff1": b(w_ff1[l]), "wff2": b(w_ff2[l]),
        "qg": q_norm_g[l][None, :], "kvg": kv_norm_g[l][None, :],
    }


def _rope_tables(pos, rope, nh, nope, prompt_layout):
    half = rope // 2
    inv = ROPE_BASE ** (-jnp.arange(half, dtype=F32) / half)
    ang = pos.astype(F32)[:, None] * inv[None, :]
    cos = jnp.concatenate([jnp.cos(ang)] * 2, axis=1)
    sin = jnp.concatenate([jnp.sin(ang)] * 2, axis=1)
    n = pos.shape[0]
    if prompt_layout:
        tc = jnp.concatenate([jnp.ones((n, nope), F32), cos, jnp.zeros((n, LANES - nope - rope), F32)], axis=1)
        ts = jnp.concatenate([jnp.zeros((n, nope), F32), sin, jnp.zeros((n, LANES - nope - rope), F32)], axis=1)
        tc, ts = jnp.tile(tc, (1, nh)), jnp.tile(ts, (1, nh))
    else:
        tc = jnp.concatenate([jnp.ones((n, nh * nope), F32), jnp.tile(cos, (1, nh))], axis=1)
        ts = jnp.concatenate([jnp.zeros((n, nh * nope), F32), jnp.tile(sin, (1, nh))], axis=1)
    return tc, ts, cos, sin


def _pick_tile(n, target):
    t = min(n, target)
    while n % t:
        t //= 2
    return t


def kernel(x_prompt, x_sample, c_prompt, c_sample, cache_sb_k, cache_sb_v, cache_mla_ckv, cache_mla_kpe,
           state_conv, page_table, norm1_g, norm2_g, final_g, w_ada, b_ada, w_in, w_o_sb, w_dw, b_dw, cln_g,
           cln_b, w_o_conv, q_norm_g, w_uq, kv_norm_g, w_uk, w_uv, w_o_mla, w_out, w_ff1, w_ff2):
    bsz, seq, d = x_prompt.shape
    nb, t_new, _ = x_sample.shape
    depth, n_pool, page, n_sb_heads, sb_dh = cache_sb_k.shape
    sbw = n_sb_heads * sb_dh
    cw, taps = w_dw.shape[2], w_dw.shape[1]
    qr, kvr, rope = w_uq.shape[1], w_uk.shape[1], cache_mla_kpe.shape[-1]
    nh, nope, vd = w_uk.shape[2], w_uk.shape[3], w_uv.shape[3]
    n_pages = page_table.shape[1]
    past_len = n_pages * page
    mla_scale = float((nope + rope) ** -0.5)
    dims = (sbw, cw, qr, kvr, rope, float(sb_dh ** -0.5))
    assert taps - 1 <= CONV_HALO and sbw % LANES == 0 and nope + rope <= LANES

    rows_p, rows_s = bsz * seq, nb * t_new
    tm_p = _pick_tile(seq, ROW_TILE)
    tm_s = _pick_tile(rows_s, ROW_TILE // 2)
    tq = _pick_tile(seq, ATTN_TILE)
    tiles_per_seq = seq // tm_p

    n_c = bsz + nb
    c_all = jnp.pad(jnp.concatenate([c_prompt, c_sample], axis=0), ((0, -n_c % 8), (0, 0)))
    mod = _mod_call(c_all, w_ada.astype(BF16), b_ada)
    mod_p = mod[:, :bsz, None, :]
    mod_s = jnp.repeat(mod[:, bsz:n_c], t_new, axis=1)

    pos_p = jnp.arange(seq, dtype=jnp.int32)
    pos_s = past_len + jnp.arange(t_new, dtype=jnp.int32)
    tabs_p = _rope_tables(pos_p, rope, nh, nope, True)
    tabs_s = tuple(jnp.tile(t, (nb, 1)) for t in _rope_tables(pos_s, rope, nh, nope, False))

    caches = (cache_sb_k.transpose(0, 1, 3, 4, 2).reshape(depth, n_pool, sbw, page),
              cache_sb_v.transpose(0, 1, 3, 4, 2).reshape(depth, n_pool, sbw, page),
              cache_mla_ckv, cache_mla_kpe.transpose(0, 1, 3, 2))
    head_of_col = jnp.arange(sbw, dtype=jnp.int32) // sb_dh
    head_mask = (head_of_col[None, :] == jnp.arange(n_sb_heads, dtype=jnp.int32)[:, None])

    xp = x_prompt.reshape(rows_p, d)
    xs = x_sample.reshape(rows_s, d)
    rows_out_p, rows_out_s = [], []
    for l in range(depth):
        lw = _prep_layer(l, dims, w_in, w_o_sb, w_o_conv, w_uq, w_uk, w_uv, w_o_mla, w_out, w_ff1, w_ff2,
                         q_norm_g, kv_norm_g)
        g1, g2 = norm1_g[l][None, :], norm2_g[l][None, :]
        conv_w = (w_dw[l], b_dw[l][None, :], cln_g[l][None, :], cln_b[l][None, :])
        final = l == depth - 1
        fg = final_g[None, :]

        lw_p = dict(lw, wqa=lw["wqa_p"], wqb=lw["wqb_p"])
        (qab, kab, vab, ka, va, u, q, kf, vc, ckv, kpe) = _in_proj_call(
            xp, mod_p[l], g1, lw_p, tabs_p, dims, True, tm_p, tiles_per_seq)
        r3 = lambda a: a.reshape(bsz, seq, a.shape[-1])
        ya = _sb_prompt_call(r3(qab), r3(kab), r3(vab), tq).reshape(rows_p, sbw)
        yc = _mla_prompt_call(r3(q), r3(kf), r3(vc), tq, mla_scale).reshape(rows_p, nh * vd)
        u4 = u.reshape(bsz, tiles_per_seq, tm_p, cw)
        halo = jnp.concatenate([jnp.zeros((bsz, 1, CONV_HALO, cw), F32), u4[:, :-1, tm_p - CONV_HALO:, :]], axis=1)
        cact = _conv_call(u4.reshape(bsz * tiles_per_seq, tm_p, cw), halo.reshape(bsz * tiles_per_seq, CONV_HALO, cw),
                          *conv_w, 1).reshape(rows_p, cw)
        xp = _merge_call(xp, mod_p[l], g1, ya, cact, yc, lw, False, tm_p, tiles_per_seq)
        xp = _ffn_call(xp, mod_p[l], g2, lw["wff1"], lw["wff2"], fg, final, False, tm_p, tiles_per_seq)
        rows_out_p.append((ka.reshape(bsz, seq, n_sb_heads, sb_dh), va.reshape(bsz, seq, n_sb_heads, sb_dh),
                           ckv.reshape(bsz, seq, kvr), kpe.reshape(bsz, seq, rope),
                           u.reshape(bsz, seq, cw)[:, seq - (taps - 1):, :]))

        lw_s = dict(lw, wqa=lw["wqa_s"], wqb=lw["wqb_s"])
        (qab, ka, va, u, qlat, qpe, ckv, kpe) = _in_proj_call(
            xs, mod_s[l], g1, lw_s, tabs_s, dims, False, tm_s, 1)
        qm = jnp.where(head_mask[None, :, None, :], qab.reshape(nb, 1, t_new, sbw), 0).reshape(
            nb, n_sb_heads * t_new, sbw).astype(BF16)
        ql = qlat.reshape(nb, t_new, nh, kvr).transpose(0, 2, 1, 3).reshape(nb, nh * t_new, kvr)
        qp = qpe.reshape(nb, t_new, nh, rope).transpose(0, 2, 1, 3).reshape(nb, nh * t_new, rope)
        padk = lambda a: jnp.pad(a.reshape(nb, t_new, a.shape[-1]).astype(BF16), ((0, 0), (0, page - t_new), (0, 0)))
        padt = lambda a: padk(a).transpose(0, 2, 1)
        ya, yc = _sample_attn_call(page_table, qm, ql, qp, padt(ka), padt(va), padk(ckv), padt(kpe), lw["wv"],
                                   caches, l, t_new, mla_scale, n_sb_heads, nh, _pick_tile(n_pages, PAGES_PER_STEP))
        u3 = u.reshape(nb, t_new, cw)
        state = state_conv[l]
        halo = jnp.pad(state, ((0, 0), (CONV_HALO - (taps - 1), 0), (0, 0)))
        cact = _conv_call(u3, halo, *conv_w, _pick_tile(nb, 32)).reshape(rows_s, cw)
        xs = _merge_call(xs, mod_s[l], g1, ya.reshape(rows_s, sbw), cact, yc.reshape(rows_s, nh * vd), lw, True,
                         tm_s, 1)
        xs = _ffn_call(xs, mod_s[l], g2, lw["wff1"], lw["wff2"], fg, final, True, tm_s, 1)
        new_state = jnp.concatenate([state, u3], axis=1)[:, -(taps - 1):, :]
        rows_out_s.append((ka.reshape(nb, t_new, n_sb_heads, sb_dh), va.reshape(nb, t_new, n_sb_heads, sb_dh),
                           ckv.reshape(nb, t_new, kvr), kpe.reshape(nb, t_new, rope), new_state))

    stk = lambda rows, i: jnp.stack([r[i] for r in rows], axis=0)
    return (xp.reshape(bsz, seq, d), xs.reshape(nb, t_new, d),
            stk(rows_out_p, 0), stk(rows_out_p, 1), stk(rows_out_p, 2), stk(rows_out_p, 3), stk(rows_out_p, 4),
            stk(rows_out_s, 0), stk(rows_out_s, 1), stk(rows_out_s, 2), stk(rows_out_s, 3), stk(rows_out_s, 4))
```

```python
import functools

import jax
import jax.numpy as jnp
from jax import lax
from jax.experimental import pallas as pl
from jax.experimental.pallas import tpu as pltpu

F32 = jnp.float32
BF16 = jnp.bfloat16

EPS = 1e-6
ROPE_BASE = 10000.0
LANES = 128
MXU_DIM = 256
LOG2E = 1.4426950408889634
SB_DEAD_LOG = -120.0
CONV_HALO = 32
VMEM_LIMIT = 56 * 1024 * 1024
ROW_TILE = 512
ATTN_TILE = 512
PAGES_PER_STEP = 16


def _cparams(sem):
    return pltpu.CompilerParams(dimension_semantics=sem, vmem_limit_bytes=VMEM_LIMIT)


def _dot(a, b):
    return jnp.dot(a, b, preferred_element_type=F32)


def _dot_nt(a, b):
    return lax.dot_general(a, b, (((1,), (1,)), ((), ())), preferred_element_type=F32)


def _rms(x, g):
    return x * lax.rsqrt(jnp.mean(x * x, axis=-1, keepdims=True) + EPS) * g


def _sigmoid(x):
    return 1.0 / (1.0 + jnp.exp(-x))


def _neg_softplus(z):
    return -(jnp.maximum(z, 0.0) + jnp.log(1.0 + jnp.exp(-jnp.abs(z))))


def _suffix_matrix(n):
    r = lax.broadcasted_iota(jnp.int32, (n, n), 0)
    c = lax.broadcasted_iota(jnp.int32, (n, n), 1)
    return jnp.where(r > c, 1.0, 0.0).astype(BF16)


def _suffix_sums(ls, u, stack):
    rows = ls[0].shape[0]
    his = [l.astype(BF16) for l in ls]
    los = [(l - hi.astype(F32)).astype(BF16) for l, hi in zip(ls, his)]
    if not stack:
        return [_dot(hi, u) + _dot(lo, u) for hi, lo in zip(his, los)]
    s = _dot(jnp.concatenate(his + los, axis=0), u)
    n = len(ls)
    return [s[g * rows:(g + 1) * rows] + s[(n + g) * rows:(n + g + 1) * rows] for g in range(n)]


def _mod_kernel(c_ref, w_ref, b_ref, o_ref):
    c = c_ref[...]
    a = (c * _sigmoid(c)).astype(BF16)
    o_ref[...] = _dot(a, w_ref[...]) + b_ref[...]


def _mod_call(c_all, w_ada, b_ada):
    depth, d, n = w_ada.shape
    rows = c_all.shape[0]
    tn = 1024
    return pl.pallas_call(
        _mod_kernel,
        out_shape=jax.ShapeDtypeStruct((depth, rows, n), F32),
        grid=(depth, n // tn),
        in_specs=[
            pl.BlockSpec((rows, d), lambda l, j: (0, 0)),
            pl.BlockSpec((None, d, tn), lambda l, j: (l, 0, j)),
            pl.BlockSpec((None, 1, tn), lambda l, j: (l, 0, j)),
        ],
        out_specs=pl.BlockSpec((None, rows, tn), lambda l, j: (l, 0, j)),
        compiler_params=_cparams(("arbitrary", "arbitrary")),
        name="ada_mod",
    )(c_all, w_ada, b_ada.reshape(depth, 1, n))


def _mod_spec(per_row, tm, d, k, tiles_per_seq):
    if per_row:
        return pl.BlockSpec((tm, d), lambda i: (i, k))
    return pl.BlockSpec((None, 1, d), lambda i: (i // tiles_per_seq, 0, k))


def _const_spec(shape):
    nd = len(shape)
    return pl.BlockSpec(shape, lambda i: (0,) * nd)


def _in_proj_kernel(dims, prompt, *refs):
    sbw, cw, qr, kvr, rope, sb_scale = dims
    if prompt:
        (x_ref, sh_ref, sc_ref, g1_ref, w1_ref, qg_ref, wqa_ref, wqb_ref, kvg_ref, tc_ref, ts_ref, kc_ref,
         ks_ref, wk_ref, ek_ref, wv_ref,
         qab_ref, kab_ref, vab_ref, ka_ref, va_ref, u_ref, q_ref, kf_ref, vc_ref, ckv_ref, kpe_ref) = refs
    else:
        (x_ref, sh_ref, sc_ref, g1_ref, w1_ref, qg_ref, wqa_ref, wqb_ref, kvg_ref, tc_ref, ts_ref, kc_ref,
         ks_ref, wabs_ref,
         qab_ref, ka_ref, va_ref, u_ref, qlat_ref, qpe_ref, ckv_ref, kpe_ref) = refs

    x = x_ref[...]
    h = _rms(x, g1_ref[...]) * (1.0 + sc_ref[...]) + sh_ref[...]
    p = _dot(h.astype(BF16), w1_ref[...])
    o = 0
    qa = p[:, o:o + sbw]; o += sbw
    ka = p[:, o:o + sbw]; o += sbw
    va = p[:, o:o + sbw]; o += sbw
    ga = p[:, o:o + cw]; o += cw
    gb = p[:, o:o + cw]; o += cw
    cq = p[:, o:o + qr]; o += qr
    ckv = p[:, o:o + kvr]; o += kvr
    kpe = p[:, o:o + LANES][:, :rope]; o += LANES
    kpe_rot = p[:, o:o + LANES][:, :rope]

    qab_ref[...] = (qa * sb_scale).astype(BF16)
    ka_ref[...] = ka
    va_ref[...] = va
    u_ref[...] = ga * _sigmoid(gb)

    cqn = _rms(cq, qg_ref[...]).astype(BF16)
    q = _dot(cqn, wqa_ref[...]) * tc_ref[...] + _dot(cqn, wqb_ref[...]) * ts_ref[...]
    ckvn = _rms(ckv, kvg_ref[...])
    kper = kpe * kc_ref[...] + kpe_rot * ks_ref[...]
    ckv_ref[...] = ckvn
    kpe_ref[...] = kper
    if prompt:
        kab_ref[...] = ka.astype(BF16)
        vab_ref[...] = va.astype(BF16)
        q_ref[...] = q.astype(BF16)
        cb = ckvn.astype(BF16)
        kf_ref[...] = (_dot(cb, wk_ref[...]) + _dot(kper.astype(BF16), ek_ref[...])).astype(BF16)
        vc_ref[...] = _dot(cb, wv_ref[...]).astype(BF16)
    else:
        nope_w = wabs_ref.shape[0]
        qlat_ref[...] = _dot(q[:, :nope_w].astype(BF16), wabs_ref[...]).astype(BF16)
        qpe_ref[...] = q[:, nope_w:].astype(BF16)


def _in_proj_call(x, mod, g1, lw, tabs, dims, prompt, tm, tiles_per_seq):
    rows, d = x.shape
    sbw, cw, qr, kvr, rope, _ = dims
    tc, ts, kc, ks = tabs
    ntab = tc.shape[0] // tm
    wq = tc.shape[1]
    per_row = not prompt
    row = lambda w: pl.BlockSpec((tm, w), lambda i: (i, 0))
    tab = lambda w: pl.BlockSpec((tm, w), lambda i: (i % ntab, 0))
    in_specs = [
        row(d),
        _mod_spec(per_row, tm, d, 0, tiles_per_seq),
        _mod_spec(per_row, tm, d, 1, tiles_per_seq),
        _const_spec((1, d)),
        _const_spec(lw["w1"].shape),
        _const_spec((1, qr)),
        _const_spec(lw["wqa"].shape),
        _const_spec(lw["wqb"].shape),
        _const_spec((1, kvr)),
        tab(wq), tab(wq), tab(rope), tab(rope),
    ]
    args = [x, mod, mod, g1, lw["w1"], lw["qg"], lw["wqa"], lw["wqb"], lw["kvg"], tc, ts, kc, ks]
    sds = jax.ShapeDtypeStruct
    if prompt:
        in_specs += [_const_spec(lw["wk"].shape), _const_spec(lw["ek"].shape), _const_spec(lw["wv"].shape)]
        args += [lw["wk"], lw["ek"], lw["wv"]]
        kfw, vcw = lw["wk"].shape[1], lw["wv"].shape[1]
        out_shape = [sds((rows, sbw), BF16)] * 3 + [sds((rows, sbw), F32)] * 2 + [
            sds((rows, cw), F32), sds((rows, wq), BF16), sds((rows, kfw), BF16), sds((rows, vcw), BF16),
            sds((rows, kvr), F32), sds((rows, rope), F32)]
        out_specs = [row(sbw)] * 5 + [row(cw), row(wq), row(kfw), row(vcw), row(kvr), row(rope)]
    else:
        in_specs += [_const_spec(lw["wabs"].shape)]
        args += [lw["wabs"]]
        nope_w, latw = lw["wabs"].shape
        out_shape = [sds((rows, sbw), BF16)] + [sds((rows, sbw), F32)] * 2 + [
            sds((rows, cw), F32), sds((rows, latw), BF16), sds((rows, wq - nope_w), BF16),
            sds((rows, kvr), F32), sds((rows, rope), F32)]
        out_specs = [row(sbw)] * 3 + [row(cw), row(latw), row(wq - nope_w), row(kvr), row(rope)]
    return pl.pallas_call(
        functools.partial(_in_proj_kernel, dims, prompt),
        out_shape=out_shape,
        grid=(rows // tm,),
        in_specs=in_specs,
        out_specs=out_specs,
        compiler_params=_cparams(("arbitrary",)),
        name="in_proj_prompt" if prompt else "in_proj_sample",
    )(*args)


def _sb_weights(z, l, carry, u, sub):
    ngroups = z.shape[1] // sub
    lgs = [l[:, g * sub:(g + 1) * sub] for g in range(ngroups)]
    es = _suffix_sums(lgs, u, False)
    for g in reversed(range(ngroups)):
        es[g] = es[g] + carry
        carry = es[g][:, :1] + lgs[g][:, :1]
    e = es[0] if ngroups == 1 else jnp.concatenate(es, axis=1)
    return jnp.exp(z + l + e), carry


def _sb_prompt_kernel(tq, sub, q_ref, k_ref, v_ref, o_ref):
    i = pl.program_id(2)
    half = q_ref.shape[1] // 2
    q = q_ref[...]
    lane = lax.broadcasted_iota(jnp.int32, (1, q.shape[1]), 1)
    first = lane < half
    zero = jnp.zeros_like(q)
    qh = (jnp.where(first, q, zero), jnp.where(first, zero, q))
    u = _suffix_matrix(sub)

    def block(j, state, masked):
        kb = k_ref[pl.ds(pl.multiple_of(j * tq, tq), tq), :]
        vb = v_ref[pl.ds(pl.multiple_of(j * tq, tq), tq), :]
        if masked:
            causal = (lax.broadcasted_iota(jnp.int32, (tq, tq), 1) < lax.broadcasted_iota(jnp.int32, (tq, tq), 0))
        new = []
        for h in range(2):
            carry, acc = state[h]
            z = _dot_nt(qh[h], kb)
            l = _neg_softplus(z)
            if masked:
                l = jnp.where(causal, l, 0.0)
            w, carry = _sb_weights(z, l, carry, u, sub)
            if masked:
                w = jnp.where(causal, w, 0.0)
            acc = acc + _dot(w.astype(BF16), vb)
            new.append((carry, acc))
        return tuple(new)

    def alive(state):
        return (jnp.max(jnp.maximum(state[0][0], state[1][0])) > SB_DEAD_LOG).astype(jnp.int32)

    def step(loop):
        t, _, state = loop
        state = block(i - 1 - t, state, False)
        return t + 1, alive(state), state

    init = tuple((jnp.zeros((tq, 1), F32), jnp.zeros((tq, q.shape[1]), F32)) for _ in range(2))
    state = block(i, init, True)
    _, _, state = lax.while_loop(lambda loop: (loop[0] < i) & (loop[1] > 0), step, (0, alive(state), state))
    o_ref[...] = jnp.where(first, state[0][1], state[1][1]).astype(o_ref.dtype)


def _sb_prompt_call(q, k, v, tq):
    b, s, w = q.shape
    pw = LANES
    npair = w // pw
    return pl.pallas_call(
        functools.partial(_sb_prompt_kernel, tq, min(tq, MXU_DIM)),
        out_shape=jax.ShapeDtypeStruct((b, s, w), BF16),
        grid=(b, npair, s // tq),
        in_specs=[
            pl.BlockSpec((None, tq, pw), lambda bi, hp, i: (bi, i, hp)),
            pl.BlockSpec((None, s, pw), lambda bi, hp, i: (bi, 0, hp)),
            pl.BlockSpec((None, s, pw), lambda bi, hp, i: (bi, 0, hp)),
        ],
        out_specs=pl.BlockSpec((None, tq, pw), lambda bi, hp, i: (bi, i, hp)),
        compiler_params=_cparams(("arbitrary", "arbitrary", "arbitrary")),
        name="sb_prompt",
    )(q, k, v)


def _mla_prompt_kernel(tq, scale, q_ref, k_ref, v_ref, o_ref):
    i = pl.program_id(2)
    vw = v_ref.shape[1]
    lane = lax.broadcasted_iota(jnp.int32, (1, vw), 1)
    first = lane < vw // 2
    qh = (q_ref[:, :LANES], q_ref[:, LANES:])
    scale2 = scale * LOG2E

    def block(j, state, masked):
        start = pl.multiple_of(j * tq, tq)
        vb = v_ref[pl.ds(start, tq), :]
        if masked:
            causal = (lax.broadcasted_iota(jnp.int32, (tq, tq), 1) <= lax.broadcasted_iota(jnp.int32, (tq, tq), 0))
        new = []
        for h in range(2):
            m, l, acc = state[h]
            kb = k_ref[pl.ds(start, tq), h * LANES:(h + 1) * LANES]
            s = _dot_nt(qh[h], kb) * scale2
            if masked:
                s = jnp.where(causal, s, -jnp.inf)
            m_new = jnp.maximum(m, jnp.max(s, axis=-1, keepdims=True))
            a = jnp.exp2(m - m_new)
            p = jnp.exp2(s - m_new)
            l = l * a + jnp.sum(p, axis=-1, keepdims=True)
            acc = acc * a + _dot(p.astype(BF16), vb)
            new.append((m_new, l, acc))
        return tuple(new)

    init = tuple((jnp.full((tq, 1), -jnp.inf, F32), jnp.zeros((tq, 1), F32), jnp.zeros((tq, vw), F32))
                 for _ in range(2))
    state = block(i, init, True)
    state = lax.fori_loop(0, i, lambda t, s: block(i - 1 - t, s, False), state)
    out = jnp.where(first, state[0][2] / state[0][1], state[1][2] / state[1][1])
    o_ref[...] = out.astype(o_ref.dtype)


def _mla_prompt_call(q, k, v, tq, scale):
    b, s, qw = q.shape
    vw = v.shape[2]
    npair = qw // (2 * LANES)
    pv = vw // npair
    return pl.pallas_call(
        functools.partial(_mla_prompt_kernel, tq, scale),
        out_shape=jax.ShapeDtypeStruct((b, s, vw), BF16),
        grid=(b, npair, s // tq),
        in_specs=[
            pl.BlockSpec((None, tq, 2 * LANES), lambda bi, hp, i: (bi, i, hp)),
            pl.BlockSpec((None, s, 2 * LANES), lambda bi, hp, i: (bi, 0, hp)),
            pl.BlockSpec((None, s, pv), lambda bi, hp, i: (bi, 0, hp)),
        ],
        out_specs=pl.BlockSpec((None, tq, pv), lambda bi, hp, i: (bi, i, hp)),
        compiler_params=_cparams(("arbitrary", "arbitrary", "arbitrary")),
        name="mla_prompt",
    )(q, k, v)


def _token_masks(shape, t_new):
    qi = lax.broadcasted_iota(jnp.int32, shape, 0) % t_new
    ki = lax.broadcasted_iota(jnp.int32, shape, 1)
    return ki < qi, ki <= qi


def _sb_chunk(qm, u, kts, vts, carry, acc, t_new=None):
    zs = [_dot(qm, kt) for kt in kts]
    ls = [_neg_softplus(z) for z in zs]
    if t_new is not None:
        valid = _token_masks(zs[0].shape, t_new)[0]
        ls = [jnp.where(valid, l, 0.0) for l in ls]
    sfx = _suffix_sums(ls, u, True)
    for z, l, e, vt in zip(zs, ls, sfx, vts):
        w = jnp.exp(z + l + (e + carry))
        if t_new is not None:
            w = jnp.where(valid, w, 0.0)
        carry = carry + (e[:, :1] + l[:, :1])
        acc = acc + _dot_nt(w.astype(BF16), vt)
    return carry, acc


def _mla_chunk(ql, qp, scale2, cbs, rts, m, l, lat, t_new=None):
    ss = [(_dot_nt(ql, cb) + _dot(qp, rt)) * scale2 for cb, rt in zip(cbs, rts)]
    if t_new is not None:
        valid = _token_masks(ss[0].shape, t_new)[1]
        ss = [jnp.where(valid, s, -jnp.inf) for s in ss]
    m_new = jnp.maximum(m, jnp.max(functools.reduce(jnp.maximum, ss), axis=-1, keepdims=True))
    a = jnp.exp2(m - m_new)
    ps = [jnp.exp2(s - m_new) for s in ss]
    l = l * a + jnp.sum(functools.reduce(jnp.add, ps), axis=-1, keepdims=True)
    lat = lat * a
    for p, cb in zip(ps, cbs):
        lat = lat + _dot(p.astype(BF16), cb)
    return m_new, l, lat


def _load_pages(pages, g_pages, which):
    return [pages[4 * g + which][...].astype(BF16) for g in range(g_pages)]


def _sample_head_kernel(cfg, pt_ref, *refs):
    g_pages, t_new, mla_scale = cfg
    qm_ref, ql_ref, qp_ref, kn_ref, vn_ref, cn_ref, rn_ref = refs[:7]
    pages = refs[7:7 + 4 * g_pages]
    carry_ref, acc_ref, m_ref, l_ref, lat_ref = refs[7 + 4 * g_pages:]
    u = _suffix_matrix(kn_ref.shape[1])
    qm, ql, qp = qm_ref[...], ql_ref[...], qp_ref[...]
    scale2 = mla_scale * LOG2E
    carry = jnp.zeros(carry_ref.shape, F32)
    acc = jnp.zeros(acc_ref.shape, F32)
    carry, acc = _sb_chunk(qm, u, [kn_ref[...]], [vn_ref[...]], carry, acc, t_new)
    carry, acc = _sb_chunk(qm, u, _load_pages(pages, g_pages, 0), _load_pages(pages, g_pages, 1), carry, acc)
    carry_ref[...] = carry
    acc_ref[...] = acc
    m = jnp.full(m_ref.shape, -jnp.inf, F32)
    l = jnp.zeros(l_ref.shape, F32)
    lat = jnp.zeros(lat_ref.shape, F32)
    m, l, lat = _mla_chunk(ql, qp, scale2, [cn_ref[...]], [rn_ref[...]], m, l, lat, t_new)
    m, l, lat = _mla_chunk(ql, qp, scale2, _load_pages(pages, g_pages, 2), _load_pages(pages, g_pages, 3), m, l, lat)
    m_ref[...] = m
    l_ref[...] = l
    lat_ref[...] = lat


def _sample_tail_kernel(cfg, pt_ref, dead_ref, *refs):
    g_pages, n_sb_heads, n_mla_heads, t_new, mla_scale = cfg
    qm_ref, ql_ref, qp_ref, wuv_ref, carry0_ref, acc0_ref, m0_ref, l0_ref, lat0_ref = refs[:9]
    pages = refs[9:9 + 4 * g_pages]
    ya_ref, yc_ref = refs[9 + 4 * g_pages:11 + 4 * g_pages]
    carry_ref, acc_ref, m_ref, l_ref, lat_ref = refs[11 + 4 * g_pages:]
    b = pl.program_id(0)
    c = pl.program_id(1)
    scale2 = mla_scale * LOG2E

    @pl.when(c == 0)
    def _():
        carry_ref[...] = carry0_ref[...]
        acc_ref[...] = acc0_ref[...]
        m_ref[...] = m0_ref[...]
        l_ref[...] = l0_ref[...]
        lat_ref[...] = lat0_ref[...]

    @pl.when(dead_ref[b] == 0)
    def _():
        u = _suffix_matrix(pages[0].shape[1])
        carry, acc = _sb_chunk(qm_ref[...], u, _load_pages(pages, g_pages, 0), _load_pages(pages, g_pages, 1),
                               carry_ref[...], acc_ref[...])
        carry_ref[...] = carry
        acc_ref[...] = acc

    m, l, lat = _mla_chunk(ql_ref[...], qp_ref[...], scale2, _load_pages(pages, g_pages, 2),
                           _load_pages(pages, g_pages, 3), m_ref[...], l_ref[...], lat_ref[...])
    m_ref[...] = m
    l_ref[...] = l
    lat_ref[...] = lat

    @pl.when(c == pl.num_programs(1) - 1)
    def _():
        acc = acc_ref[...]
        w = acc.shape[1]
        hsel = (lax.broadcasted_iota(jnp.int32, acc.shape, 1) // (w // n_sb_heads)
                == lax.broadcasted_iota(jnp.int32, acc.shape, 0) // t_new)
        ya_ref[...] = jnp.sum(jnp.where(hsel, acc, 0.0).reshape(n_sb_heads, t_new, w), axis=0)
        lat = (lat_ref[...] / l_ref[...]).astype(BF16)
        full = _dot(lat, wuv_ref[...])
        vw = full.shape[1]
        msel = (lax.broadcasted_iota(jnp.int32, full.shape, 1) // (vw // n_mla_heads)
                == lax.broadcasted_iota(jnp.int32, full.shape, 0) // t_new)
        yc_ref[...] = jnp.sum(jnp.where(msel, full, 0.0).reshape(n_mla_heads, t_new, vw), axis=0)


def _sample_attn_call(page_table, qm, ql, qp, kn, vn, cn, rn, wuv, caches, layer, t_new, mla_scale,
                      n_sb_heads, n_mla_heads, g_pages):
    nb, n_pages = page_table.shape
    sbw, kvr = caches[0].shape[2], caches[2].shape[3]
    vw = wuv.shape[1]
    assert n_pages >= 2
    head_pages = _pick_tile(n_pages // 2, g_pages)
    g_pages = _pick_tile(n_pages - head_pages, g_pages)
    steps = (n_pages - head_pages) // g_pages
    pt_flat = page_table.reshape(-1)
    rows_sb, rows_mla = qm.shape[1], ql.shape[1]
    state_shapes = [(rows_sb, 1), (rows_sb, sbw), (rows_mla, 1), (rows_mla, 1), (rows_mla, kvr)]
    sds = jax.ShapeDtypeStruct

    def seq1(a):
        return pl.BlockSpec((None,) + a.shape[1:], lambda b, pt: (b,) + (0,) * (a.ndim - 1))

    def head_page(cache, g):
        return pl.BlockSpec((None, None) + cache.shape[2:],
                            lambda b, pt: (layer, pt[b * n_pages + n_pages - 1 - g], 0, 0))

    head_in = [qm, ql, qp, kn, vn, cn, rn]
    in_specs = [seq1(a) for a in head_in]
    args = list(head_in)
    for g in range(head_pages):
        in_specs += [head_page(cache, g) for cache in caches]
        args += list(caches)
    state = pl.pallas_call(
        functools.partial(_sample_head_kernel, (head_pages, t_new, mla_scale)),
        out_shape=[sds((nb,) + s, F32) for s in state_shapes],
        grid_spec=pltpu.PrefetchScalarGridSpec(
            num_scalar_prefetch=1, grid=(nb,), in_specs=in_specs,
            out_specs=[pl.BlockSpec((None,) + s, lambda b, pt: (b, 0, 0)) for s in state_shapes]),
        compiler_params=_cparams(("arbitrary",)),
        name="sample_head",
    )(pt_flat, *args)

    dead = jnp.all(state[0] < SB_DEAD_LOG, axis=(1, 2)).astype(jnp.int32)

    def seq2(a):
        return pl.BlockSpec((None,) + a.shape[1:], lambda b, c, pt, dd: (b,) + (0,) * (a.ndim - 1))

    def tail_page(cache, g, is_sb):
        def index(b, c, pt, dd):
            p = pt[b * n_pages + n_pages - 1 - head_pages - (c * g_pages + g)]
            if is_sb:
                p = jnp.where(dd[b] != 0, 0, p)
            return (layer, p, 0, 0)
        return pl.BlockSpec((None, None) + cache.shape[2:], index)

    tail_in = [qm, ql, qp]
    in_specs = [seq2(a) for a in tail_in] + [pl.BlockSpec(wuv.shape, lambda b, c, pt, dd: (0, 0))]
    in_specs += [seq2(a) for a in state]
    args = tail_in + [wuv] + list(state)
    for g in range(g_pages):
        in_specs += [tail_page(cache, g, k < 2) for k, cache in enumerate(caches)]
        args += list(caches)
    return pl.pallas_call(
        functools.partial(_sample_tail_kernel, (g_pages, n_sb_heads, n_mla_heads, t_new, mla_scale)),
        out_shape=[sds((nb, t_new, sbw), F32), sds((nb, t_new, vw), F32)],
        grid_spec=pltpu.PrefetchScalarGridSpec(
            num_scalar_prefetch=2, grid=(nb, steps), in_specs=in_specs,
            out_specs=[pl.BlockSpec((None, t_new, sbw), lambda b, c, pt, dd: (b, 0, 0)),
                       pl.BlockSpec((None, t_new, vw), lambda b, c, pt, dd: (b, 0, 0))],
            scratch_shapes=[pltpu.VMEM(s, F32) for s in state_shapes]),
        compiler_params=_cparams(("arbitrary", "arbitrary")),
        name="sample_tail",
    )(pt_flat, dead, *args)


def _conv_kernel(taps, u_ref, halo_ref, w_ref, b_ref, g_ref, beta_ref, o_ref, scr_ref):
    tt = u_ref.shape[1]
    scr_ref[:, :CONV_HALO, :] = halo_ref[...]
    scr_ref[:, CONV_HALO:, :] = u_ref[...]
    first = CONV_HALO - (taps - 1)
    y = jnp.zeros(u_ref.shape, F32)
    for j in range(taps):
        y = y + scr_ref[:, first + j:first + j + tt, :] * w_ref[j:j + 1, :]
    y = y + b_ref[...]
    mu = jnp.mean(y, axis=-1, keepdims=True)
    yc = y - mu
    var = jnp.mean(yc * yc, axis=-1, keepdims=True)
    y = yc * lax.rsqrt(var + EPS) * g_ref[...] + beta_ref[...]
    o_ref[...] = (y * _sigmoid(y)).astype(o_ref.dtype)


def _conv_call(u3, halo, w_dw, b_dw, g, beta, tb):
    nb, tt, cw = u3.shape
    taps = w_dw.shape[0]
    blk = lambda t: pl.BlockSpec((tb, t, cw), lambda i: (i, 0, 0))
    return pl.pallas_call(
        functools.partial(_conv_kernel, taps),
        out_shape=jax.ShapeDtypeStruct((nb, tt, cw), BF16),
        grid=(nb // tb,),
        in_specs=[blk(tt), blk(CONV_HALO), _const_spec((taps, cw)), _const_spec((1, cw)), _const_spec((1, cw)),
                  _const_spec((1, cw))],
        out_specs=blk(tt),
        scratch_shapes=[pltpu.VMEM((tb, CONV_HALO + tt, cw), F32)],
        compiler_params=_cparams(("arbitrary",)),
        name="conv_module",
    )(u3, halo, w_dw, b_dw, g, beta)


def _merge_kernel(x_ref, sh_ref, sc_ref, gt_ref, g1_ref, ya_ref, cact_ref, yc_ref, wg_ref, wsb_ref, wcv_ref,
                  wml_ref, wout_ref, o_ref):
    x = x_ref[...]
    d = x.shape[1]
    h = (_rms(x, g1_ref[...]) * (1.0 + sc_ref[...]) + sh_ref[...]).astype(BF16)
    branches = (
        _dot(ya_ref[...].astype(BF16), wsb_ref[...]),
        _dot(cact_ref[...].astype(BF16), wcv_ref[...]),
        _dot(yc_ref[...].astype(BF16), wml_ref[...]),
    )
    merged = jnp.zeros_like(x)
    for k, br in enumerate(branches):
        merged = merged + _sigmoid(_dot(h, wg_ref[:, k * d:(k + 1) * d])) * br
    o_ref[...] = x + gt_ref[...] * _dot(merged.astype(BF16), wout_ref[...])


def _merge_call(x, mod, g1, ya, cact, yc, lw, per_row, tm, tiles_per_seq):
    rows, d = x.shape
    row = lambda w: pl.BlockSpec((tm, w), lambda i: (i, 0))
    ws = [lw["wg"], lw["wsb"], lw["wcv"], lw["wml"], lw["wout"]]
    return pl.pallas_call(
        _merge_kernel,
        out_shape=jax.ShapeDtypeStruct((rows, d), F32),
        grid=(rows // tm,),
        in_specs=[row(d)] + [_mod_spec(per_row, tm, d, k, tiles_per_seq) for k in (0, 1, 2)]
        + [_const_spec((1, d)), row(ya.shape[1]), row(cact.shape[1]), row(yc.shape[1])]
        + [_const_spec(w.shape) for w in ws],
        out_specs=row(d),
        compiler_params=_cparams(("arbitrary",)),
        name="merge_out",
    )(x, mod, mod, mod, g1, ya, cact, yc, *ws)


def _ffn_kernel(chunk, final, x_ref, sh_ref, sc_ref, gt_ref, g2_ref, w1_ref, w2_ref, fg_ref, o_ref):
    x = x_ref[...]
    h = (_rms(x, g2_ref[...]) * (1.0 + sc_ref[...]) + sh_ref[...]).astype(BF16)
    acc = jnp.zeros_like(x)
    for c in range(w1_ref.shape[1] // chunk):
        t = jnp.maximum(_dot(h, w1_ref[:, c * chunk:(c + 1) * chunk]), 0.0)
        acc = acc + _dot((t * t).astype(BF16), w2_ref[c * chunk:(c + 1) * chunk, :])
    y = x + gt_ref[...] * acc
    if final:
        y = _rms(y, fg_ref[...])
    o_ref[...] = y


def _ffn_call(x, mod, g2, w1, w2, fg, final, per_row, tm, tiles_per_seq):
    rows, d = x.shape
    row = pl.BlockSpec((tm, d), lambda i: (i, 0))
    return pl.pallas_call(
        functools.partial(_ffn_kernel, 512, final),
        out_shape=jax.ShapeDtypeStruct((rows, d), F32),
        grid=(rows // tm,),
        in_specs=[row] + [_mod_spec(per_row, tm, d, k, tiles_per_seq) for k in (3, 4, 5)]
        + [_const_spec((1, d)), _const_spec(w1.shape), _const_spec(w2.shape), _const_spec((1, d))],
        out_specs=row,
        compiler_params=_cparams(("arbitrary",)),
        name="ffn_final" if final else "ffn",
    )(x, mod, mod, mod, g2, w1, w2, fg)


def _rot_cols(w):
    half = w.shape[-1] // 2
    return jnp.concatenate([-w[..., half:], w[..., :half]], axis=-1)


def _pad_cols(w, width):
    return jnp.pad(w, [(0, 0)] * (w.ndim - 1) + [(0, width - w.shape[-1])])


def _prep_layer(l, dims, w_in, w_o_sb, w_o_conv, w_uq, w_uk, w_uv, w_o_mla, w_out, w_ff1, w_ff2,
                q_norm_g, kv_norm_g):
    sbw, cw, qr, kvr, rope, _ = dims
    d = w_in.shape[1]
    nh, nope, vd = w_uk.shape[2], w_uk.shape[3], w_uv.shape[3]
    wi = w_in[l]
    o = 3 * sbw + 2 * cw + qr + kvr
    w_kpe = wi[:, o:o + rope]
    w1 = jnp.concatenate([wi[:, :o], _pad_cols(w_kpe, LANES), _pad_cols(_rot_cols(w_kpe), LANES)], axis=1)
    wg = wi[:, o + rope:]
    wq = w_uq[l].reshape(qr, nh, nope + rope)
    wq_nope, wq_pe = wq[..., :nope], wq[..., nope:]
    wq_rot = _rot_cols(wq_pe)
    zn = jnp.zeros_like(wq_nope)
    wqa_p = _pad_cols(jnp.concatenate([wq_nope, wq_pe], -1), LANES).reshape(qr, nh * LANES)
    wqb_p = _pad_cols(jnp.concatenate([zn, wq_rot], -1), LANES).reshape(qr, nh * LANES)
    wqa_s = jnp.concatenate([wq_nope.reshape(qr, -1), wq_pe.reshape(qr, -1)], axis=1)
    wqb_s = jnp.concatenate([zn.reshape(qr, -1), wq_rot.reshape(qr, -1)], axis=1)
    wk = _pad_cols(w_uk[l], LANES).reshape(kvr, nh * LANES)
    ek_head = jnp.concatenate([jnp.zeros((rope, nope), F32), jnp.eye(rope, dtype=F32),
                               jnp.zeros((rope, LANES - nope - rope), F32)], axis=1)
    ek = jnp.tile(ek_head, (1, nh))
    wv = w_uv[l].reshape(kvr, nh * vd)
    wabs = jnp.einsum('chd,hg->hdgc', w_uk[l], jnp.eye(nh, dtype=F32)).reshape(nh * nope, nh * kvr)
    b = lambda a: a.astype(BF16)
    return {
        "w1": b(w1), "wg": b(wg), "wqa_p": b(wqa_p), "wqb_p": b(wqb_p), "wqa_s": b(wqa_s), "wqb_s": b(wqb_s),
        "wk": b(wk), "ek": b(ek), "wv": b(wv), "wabs": b(wabs),
        "wsb": b(w_o_sb[l]), "wcv": b(w_o_conv[l]), "wml": b(w_o_mla[l]), "wout": b(w_out[l]),
        "wff1": b(w_ff1[l]), "wff2": b(w_ff2[l]),
        "qg": q_norm_g[l][None, :], "kvg": kv_norm_g[l][None, :],
    }


def _rope_tables(pos, rope, nh, nope, prompt_layout):
    half = rope // 2
    inv = ROPE_BASE ** (-jnp.arange(half, dtype=F32) / half)
    ang = pos.astype(F32)[:, None] * inv[None, :]
    cos = jnp.concatenate([jnp.cos(ang)] * 2, axis=1)
    sin = jnp.concatenate([jnp.sin(ang)] * 2, axis=1)
    n = pos.shape[0]
    if prompt_layout:
        tc = jnp.concatenate([jnp.ones((n, nope), F32), cos, jnp.zeros((n, LANES - nope - rope), F32)], axis=1)
        ts = jnp.concatenate([jnp.zeros((n, nope), F32), sin, jnp.zeros((n, LANES - nope - rope), F32)], axis=1)
        tc, ts = jnp.tile(tc, (1, nh)), jnp.tile(ts, (1, nh))
    else:
        tc = jnp.concatenate([jnp.ones((n, nh * nope), F32), jnp.tile(cos, (1, nh))], axis=1)
        ts = jnp.concatenate([jnp.zeros((n, nh * nope), F32), jnp.tile(sin, (1, nh))], axis=1)
    return tc, ts, cos, sin


def _pick_tile(n, target):
    t = min(n, target)
    while n % t:
        t //= 2
    return t


def kernel(x_prompt, x_sample, c_prompt, c_sample, cache_sb_k, cache_sb_v, cache_mla_ckv, cache_mla_kpe,
           state_conv, page_table, norm1_g, norm2_g, final_g, w_ada, b_ada, w_in, w_o_sb, w_dw, b_dw, cln_g,
           cln_b, w_o_conv, q_norm_g, w_uq, kv_norm_g, w_uk, w_uv, w_o_mla, w_out, w_ff1, w_ff2):
    bsz, seq, d = x_prompt.shape
    nb, t_new, _ = x_sample.shape
    depth, n_pool, page, n_sb_heads, sb_dh = cache_sb_k.shape
    sbw = n_sb_heads * sb_dh
    cw, taps = w_dw.shape[2], w_dw.shape[1]
    qr, kvr, rope = w_uq.shape[1], w_uk.shape[1], cache_mla_kpe.shape[-1]
    nh, nope, vd = w_uk.shape[2], w_uk.shape[3], w_uv.shape[3]
    n_pages = page_table.shape[1]
    past_len = n_pages * page
    mla_scale = float((nope + rope) ** -0.5)
    dims = (sbw, cw, qr, kvr, rope, float(sb_dh ** -0.5))
    assert taps - 1 <= CONV_HALO and sbw % LANES == 0 and nope + rope <= LANES

    rows_p, rows_s = bsz * seq, nb * t_new
    tm_p = _pick_tile(seq, ROW_TILE)
    tm_s = _pick_tile(rows_s, ROW_TILE // 2)
    tq = _pick_tile(seq, ATTN_TILE)
    tiles_per_seq = seq // tm_p

    n_c = bsz + nb
    c_all = jnp.pad(jnp.concatenate([c_prompt, c_sample], axis=0), ((0, -n_c % 8), (0, 0)))
    mod = _mod_call(c_all, w_ada.astype(BF16), b_ada)
    mod_p = mod[:, :bsz, None, :]
    mod_s = jnp.repeat(mod[:, bsz:n_c], t_new, axis=1)

    pos_p = jnp.arange(seq, dtype=jnp.int32)
    pos_s = past_len + jnp.arange(t_new, dtype=jnp.int32)
    tabs_p = _rope_tables(pos_p, rope, nh, nope, True)
    tabs_s = tuple(jnp.tile(t, (nb, 1)) for t in _rope_tables(pos_s, rope, nh, nope, False))

    caches = (cache_sb_k.transpose(0, 1, 3, 4, 2).reshape(depth, n_pool, sbw, page),
              cache_sb_v.transpose(0, 1, 3, 4, 2).reshape(depth, n_pool, sbw, page),
              cache_mla_ckv, cache_mla_kpe.transpose(0, 1, 3, 2))
    head_of_col = jnp.arange(sbw, dtype=jnp.int32) // sb_dh
    head_mask = (head_of_col[None, :] == jnp.arange(n_sb_heads, dtype=jnp.int32)[:, None])

    xp = x_prompt.reshape(rows_p, d)
    xs = x_sample.reshape(rows_s, d)
    rows_out_p, rows_out_s = [], []
    for l in range(depth):
        lw = _prep_layer(l, dims, w_in, w_o_sb, w_o_conv, w_uq, w_uk, w_uv, w_o_mla, w_out, w_ff1, w_ff2,
                         q_norm_g, kv_norm_g)
        g1, g2 = norm1_g[l][None, :], norm2_g[l][None, :]
        conv_w = (w_dw[l], b_dw[l][None, :], cln_g[l][None, :], cln_b[l][None, :])
        final = l == depth - 1
        fg = final_g[None, :]

        lw_p = dict(lw, wqa=lw["wqa_p"], wqb=lw["wqb_p"])
        (qab, kab, vab, ka, va, u, q, kf, vc, ckv, kpe) = _in_proj_call(
            xp, mod_p[l], g1, lw_p, tabs_p, dims, True, tm_p, tiles_per_seq)
        r3 = lambda a: a.reshape(bsz, seq, a.shape[-1])
        ya = _sb_prompt_call(r3(qab), r3(kab), r3(vab), tq).reshape(rows_p, sbw)
        yc = _mla_prompt_call(r3(q), r3(kf), r3(vc), tq, mla_scale).reshape(rows_p, nh * vd)
        u4 = u.reshape(bsz, tiles_per_seq, tm_p, cw)
        halo = jnp.concatenate([jnp.zeros((bsz, 1, CONV_HALO, cw), F32), u4[:, :-1, tm_p - CONV_HALO:, :]], axis=1)
        cact = _conv_call(u4.reshape(bsz * tiles_per_seq, tm_p, cw), halo.reshape(bsz * tiles_per_seq, CONV_HALO, cw),
                          *conv_w, 1).reshape(rows_p, cw)
        xp = _merge_call(xp, mod_p[l], g1, ya, cact, yc, lw, False, tm_p, tiles_per_seq)
        xp = _ffn_call(xp, mod_p[l], g2, lw["wff1"], lw["wff2"], fg, final, False, tm_p, tiles_per_seq)
        rows_out_p.append((ka.reshape(bsz, seq, n_sb_heads, sb_dh), va.reshape(bsz, seq, n_sb_heads, sb_dh),
                           ckv.reshape(bsz, seq, kvr), kpe.reshape(bsz, seq, rope),
                           u.reshape(bsz, seq, cw)[:, seq - (taps - 1):, :]))

        lw_s = dict(lw, wqa=lw["wqa_s"], wqb=lw["wqb_s"])
        (qab, ka, va, u, qlat, qpe, ckv, kpe) = _in_proj_call(
            xs, mod_s[l], g1, lw_s, tabs_s, dims, False, tm_s, 1)
        qm = jnp.where(head_mask[None, :, None, :], qab.reshape(nb, 1, t_new, sbw), 0).reshape(
            nb, n_sb_heads * t_new, sbw).astype(BF16)
        ql = qlat.reshape(nb, t_new, nh, kvr).transpose(0, 2, 1, 3).reshape(nb, nh * t_new, kvr)
        qp = qpe.reshape(nb, t_new, nh, rope).transpose(0, 2, 1, 3).reshape(nb, nh * t_new, rope)
        padk = lambda a: jnp.pad(a.reshape(nb, t_new, a.shape[-1]).astype(BF16), ((0, 0), (0, page - t_new), (0, 0)))
        padt = lambda a: padk(a).transpose(0, 2, 1)
        ya, yc = _sample_attn_call(page_table, qm, ql, qp, padt(ka), padt(va), padk(ckv), padt(kpe), lw["wv"],
                                   caches, l, t_new, mla_scale, n_sb_heads, nh, PAGES_PER_STEP)
        u3 = u.reshape(nb, t_new, cw)
        state = state_conv[l]
        halo = jnp.pad(state, ((0, 0), (CONV_HALO - (taps - 1), 0), (0, 0)))
        cact = _conv_call(u3, halo, *conv_w, _pick_tile(nb, 32)).reshape(rows_s, cw)
        xs = _merge_call(xs, mod_s[l], g1, ya.reshape(rows_s, sbw), cact, yc.reshape(rows_s, nh * vd), lw, True,
                         tm_s, 1)
        xs = _ffn_call(xs, mod_s[l], g2, lw["wff1"], lw["wff2"], fg, final, True, tm_s, 1)
        new_state = jnp.concatenate([state, u3], axis=1)[:, -(taps - 1):, :]
        rows_out_s.append((ka.reshape(nb, t_new, n_sb_heads, sb_dh), va.reshape(nb, t_new, n_sb_heads, sb_dh),
                           ckv.reshape(nb, t_new, kvr), kpe.reshape(nb, t_new, rope), new_state))

    stk = lambda rows, i: jnp.stack([r[i] for r in rows], axis=0)
    return (xp.reshape(bsz, seq, d), xs.reshape(nb, t_new, d),
            stk(rows_out_p, 0), stk(rows_out_p, 1), stk(rows_out_p, 2), stk(rows_out_p, 3), stk(rows_out_p, 4),
            stk(rows_out_s, 0), stk(rows_out_s, 1), stk(rows_out_s, 2), stk(rows_out_s, 3), stk(rows_out_s, 4))
```

```python
import functools

import jax
import jax.numpy as jnp
from jax import lax
from jax.experimental import pallas as pl
from jax.experimental.pallas import tpu as pltpu

F32 = jnp.float32
BF16 = jnp.bfloat16

EPS = 1e-6
ROPE_BASE = 10000.0
LANES = 128
MXU_DIM = 256
LOG2E = 1.4426950408889634
SB_DEAD_LOG = -120.0
CONV_HALO = 32
VMEM_LIMIT = 56 * 1024 * 1024
ROW_TILE = 512
ATTN_TILE = 512
PAGES_PER_STEP = 16
MLA_HEADS_PER_STEP = 2


def _cparams(sem):
    return pltpu.CompilerParams(dimension_semantics=sem, vmem_limit_bytes=VMEM_LIMIT)


def _dot(a, b):
    return jnp.dot(a, b, preferred_element_type=F32)


def _dot_nt(a, b):
    return lax.dot_general(a, b, (((1,), (1,)), ((), ())), preferred_element_type=F32)


def _rms(x, g):
    return x * lax.rsqrt(jnp.mean(x * x, axis=-1, keepdims=True) + EPS) * g


def _sigmoid(x):
    return 1.0 / (1.0 + jnp.exp(-x))


def _neg_softplus(z):
    return -(jnp.maximum(z, 0.0) + jnp.log(1.0 + jnp.exp(-jnp.abs(z))))


def _suffix_matrix(n):
    r = lax.broadcasted_iota(jnp.int32, (n, n), 0)
    c = lax.broadcasted_iota(jnp.int32, (n, n), 1)
    return jnp.where(r > c, 1.0, 0.0).astype(BF16)


def _suffix_sums(ls, u, stack):
    rows = ls[0].shape[0]
    his = [l.astype(BF16) for l in ls]
    los = [(l - hi.astype(F32)).astype(BF16) for l, hi in zip(ls, his)]
    if not stack:
        return [_dot(hi, u) + _dot(lo, u) for hi, lo in zip(his, los)]
    s = _dot(jnp.concatenate(his + los, axis=0), u)
    n = len(ls)
    return [s[g * rows:(g + 1) * rows] + s[(n + g) * rows:(n + g + 1) * rows] for g in range(n)]


def _mod_kernel(c_ref, w_ref, b_ref, o_ref):
    c = c_ref[...]
    a = (c * _sigmoid(c)).astype(BF16)
    o_ref[...] = _dot(a, w_ref[...]) + b_ref[...]


def _mod_call(c_all, w_ada, b_ada):
    depth, d, n = w_ada.shape
    rows = c_all.shape[0]
    tn = 1024
    return pl.pallas_call(
        _mod_kernel,
        out_shape=jax.ShapeDtypeStruct((depth, rows, n), F32),
        grid=(depth, n // tn),
        in_specs=[
            pl.BlockSpec((rows, d), lambda l, j: (0, 0)),
            pl.BlockSpec((None, d, tn), lambda l, j: (l, 0, j)),
            pl.BlockSpec((None, 1, tn), lambda l, j: (l, 0, j)),
        ],
        out_specs=pl.BlockSpec((None, rows, tn), lambda l, j: (l, 0, j)),
        compiler_params=_cparams(("arbitrary", "arbitrary")),
        name="ada_mod",
    )(c_all, w_ada, b_ada.reshape(depth, 1, n))


def _mod_spec(per_row, tm, d, k, tiles_per_seq):
    if per_row:
        return pl.BlockSpec((tm, d), lambda i: (i, k))
    return pl.BlockSpec((None, 1, d), lambda i: (i // tiles_per_seq, 0, k))


def _const_spec(shape):
    nd = len(shape)
    return pl.BlockSpec(shape, lambda i: (0,) * nd)


def _in_proj_kernel(dims, prompt, *refs):
    sbw, cw, qr, kvr, rope, sb_scale = dims
    if prompt:
        (x_ref, sh_ref, sc_ref, g1_ref, w1_ref, qg_ref, wqa_ref, wqb_ref, kvg_ref, tc_ref, ts_ref, kc_ref,
         ks_ref, wk_ref, ek_ref, wv_ref,
         qab_ref, kab_ref, vab_ref, ka_ref, va_ref, u_ref, q_ref, kf_ref, vc_ref, ckv_ref, kpe_ref) = refs
    else:
        (x_ref, sh_ref, sc_ref, g1_ref, w1_ref, qg_ref, wqa_ref, wqb_ref, kvg_ref, tc_ref, ts_ref, kc_ref,
         ks_ref, wabs_ref,
         qab_ref, ka_ref, va_ref, u_ref, qlat_ref, qpe_ref, ckv_ref, kpe_ref) = refs

    x = x_ref[...]
    h = _rms(x, g1_ref[...]) * (1.0 + sc_ref[...]) + sh_ref[...]
    p = _dot(h.astype(BF16), w1_ref[...])
    o = 0
    qa = p[:, o:o + sbw]; o += sbw
    ka = p[:, o:o + sbw]; o += sbw
    va = p[:, o:o + sbw]; o += sbw
    ga = p[:, o:o + cw]; o += cw
    gb = p[:, o:o + cw]; o += cw
    cq = p[:, o:o + qr]; o += qr
    ckv = p[:, o:o + kvr]; o += kvr
    kpe = p[:, o:o + LANES][:, :rope]; o += LANES
    kpe_rot = p[:, o:o + LANES][:, :rope]

    qab_ref[...] = (qa * sb_scale).astype(BF16)
    ka_ref[...] = ka
    va_ref[...] = va
    u_ref[...] = ga * _sigmoid(gb)

    cqn = _rms(cq, qg_ref[...]).astype(BF16)
    q = _dot(cqn, wqa_ref[...]) * tc_ref[...] + _dot(cqn, wqb_ref[...]) * ts_ref[...]
    ckvn = _rms(ckv, kvg_ref[...])
    kper = kpe * kc_ref[...] + kpe_rot * ks_ref[...]
    ckv_ref[...] = ckvn
    kpe_ref[...] = kper
    if prompt:
        kab_ref[...] = ka.astype(BF16)
        vab_ref[...] = va.astype(BF16)
        q_ref[...] = q.astype(BF16)
        cb = ckvn.astype(BF16)
        kf_ref[...] = (_dot(cb, wk_ref[...]) + _dot(kper.astype(BF16), ek_ref[...])).astype(BF16)
        vc_ref[...] = _dot(cb, wv_ref[...]).astype(BF16)
    else:
        nope_w = wabs_ref.shape[0]
        qlat_ref[...] = _dot(q[:, :nope_w].astype(BF16), wabs_ref[...]).astype(BF16)
        qpe_ref[...] = q[:, nope_w:].astype(BF16)


def _in_proj_call(x, mod, g1, lw, tabs, dims, prompt, tm, tiles_per_seq):
    rows, d = x.shape
    sbw, cw, qr, kvr, rope, _ = dims
    tc, ts, kc, ks = tabs
    ntab = tc.shape[0] // tm
    wq = tc.shape[1]
    per_row = not prompt
    row = lambda w: pl.BlockSpec((tm, w), lambda i: (i, 0))
    tab = lambda w: pl.BlockSpec((tm, w), lambda i: (i % ntab, 0))
    in_specs = [
        row(d),
        _mod_spec(per_row, tm, d, 0, tiles_per_seq),
        _mod_spec(per_row, tm, d, 1, tiles_per_seq),
        _const_spec((1, d)),
        _const_spec(lw["w1"].shape),
        _const_spec((1, qr)),
        _const_spec(lw["wqa"].shape),
        _const_spec(lw["wqb"].shape),
        _const_spec((1, kvr)),
        tab(wq), tab(wq), tab(rope), tab(rope),
    ]
    args = [x, mod, mod, g1, lw["w1"], lw["qg"], lw["wqa"], lw["wqb"], lw["kvg"], tc, ts, kc, ks]
    sds = jax.ShapeDtypeStruct
    if prompt:
        in_specs += [_const_spec(lw["wk"].shape), _const_spec(lw["ek"].shape), _const_spec(lw["wv"].shape)]
        args += [lw["wk"], lw["ek"], lw["wv"]]
        kfw, vcw = lw["wk"].shape[1], lw["wv"].shape[1]
        out_shape = [sds((rows, sbw), BF16)] * 3 + [sds((rows, sbw), F32)] * 2 + [
            sds((rows, cw), F32), sds((rows, wq), BF16), sds((rows, kfw), BF16), sds((rows, vcw), BF16),
            sds((rows, kvr), F32), sds((rows, rope), F32)]
        out_specs = [row(sbw)] * 5 + [row(cw), row(wq), row(kfw), row(vcw), row(kvr), row(rope)]
    else:
        in_specs += [_const_spec(lw["wabs"].shape)]
        args += [lw["wabs"]]
        nope_w, latw = lw["wabs"].shape
        out_shape = [sds((rows, sbw), BF16)] + [sds((rows, sbw), F32)] * 2 + [
            sds((rows, cw), F32), sds((rows, latw), BF16), sds((rows, wq - nope_w), BF16),
            sds((rows, kvr), F32), sds((rows, rope), F32)]
        out_specs = [row(sbw)] * 3 + [row(cw), row(latw), row(wq - nope_w), row(kvr), row(rope)]
    return pl.pallas_call(
        functools.partial(_in_proj_kernel, dims, prompt),
        out_shape=out_shape,
        grid=(rows // tm,),
        in_specs=in_specs,
        out_specs=out_specs,
        compiler_params=_cparams(("arbitrary",)),
        name="in_proj_prompt" if prompt else "in_proj_sample",
    )(*args)


def _sb_weights(z, l, carry, u, sub):
    ngroups = z.shape[1] // sub
    lgs = [l[:, g * sub:(g + 1) * sub] for g in range(ngroups)]
    es = _suffix_sums(lgs, u, False)
    for g in reversed(range(ngroups)):
        es[g] = es[g] + carry
        carry = es[g][:, :1] + lgs[g][:, :1]
    e = es[0] if ngroups == 1 else jnp.concatenate(es, axis=1)
    return jnp.exp(z + l + e), carry


def _sb_prompt_kernel(tq, sub, q_ref, k_ref, v_ref, o_ref):
    i = pl.program_id(2)
    half = q_ref.shape[1] // 2
    q = q_ref[...]
    lane = lax.broadcasted_iota(jnp.int32, (1, q.shape[1]), 1)
    first = lane < half
    zero = jnp.zeros_like(q)
    qh = (jnp.where(first, q, zero), jnp.where(first, zero, q))
    u = _suffix_matrix(sub)

    def block(j, state, masked):
        kb = k_ref[pl.ds(pl.multiple_of(j * tq, tq), tq), :]
        vb = v_ref[pl.ds(pl.multiple_of(j * tq, tq), tq), :]
        if masked:
            causal = (lax.broadcasted_iota(jnp.int32, (tq, tq), 1) < lax.broadcasted_iota(jnp.int32, (tq, tq), 0))
        new = []
        for h in range(2):
            carry, acc = state[h]
            z = _dot_nt(qh[h], kb)
            l = _neg_softplus(z)
            if masked:
                l = jnp.where(causal, l, 0.0)
            w, carry = _sb_weights(z, l, carry, u, sub)
            if masked:
                w = jnp.where(causal, w, 0.0)
            acc = acc + _dot(w.astype(BF16), vb)
            new.append((carry, acc))
        return tuple(new)

    def alive(state):
        return (jnp.max(jnp.maximum(state[0][0], state[1][0])) > SB_DEAD_LOG).astype(jnp.int32)

    def step(loop):
        t, _, state = loop
        state = block(i - 1 - t, state, False)
        return t + 1, alive(state), state

    init = tuple((jnp.zeros((tq, 1), F32), jnp.zeros((tq, q.shape[1]), F32)) for _ in range(2))
    state = block(i, init, True)
    _, _, state = lax.while_loop(lambda loop: (loop[0] < i) & (loop[1] > 0), step, (0, alive(state), state))
    o_ref[...] = jnp.where(first, state[0][1], state[1][1]).astype(o_ref.dtype)


def _sb_prompt_call(q, k, v, tq):
    b, s, w = q.shape
    pw = LANES
    npair = w // pw
    return pl.pallas_call(
        functools.partial(_sb_prompt_kernel, tq, min(tq, MXU_DIM)),
        out_shape=jax.ShapeDtypeStruct((b, s, w), BF16),
        grid=(b, npair, s // tq),
        in_specs=[
            pl.BlockSpec((None, tq, pw), lambda bi, hp, i: (bi, i, hp)),
            pl.BlockSpec((None, s, pw), lambda bi, hp, i: (bi, 0, hp)),
            pl.BlockSpec((None, s, pw), lambda bi, hp, i: (bi, 0, hp)),
        ],
        out_specs=pl.BlockSpec((None, tq, pw), lambda bi, hp, i: (bi, i, hp)),
        compiler_params=_cparams(("arbitrary", "arbitrary", "arbitrary")),
        name="sb_prompt",
    )(q, k, v)


def _mla_prompt_kernel(tq, scale, heads, q_ref, k_ref, v_ref, o_ref):
    i = pl.program_id(2)
    lane = lax.broadcasted_iota(jnp.int32, (1, LANES), 1)
    first = lane < LANES // 2
    qh = [q_ref[:, h * LANES:(h + 1) * LANES] for h in range(heads)]
    scale2 = scale * LOG2E

    def block(j, state, masked):
        start = pl.multiple_of(j * tq, tq)
        if masked:
            causal = (lax.broadcasted_iota(jnp.int32, (tq, tq), 1) <= lax.broadcasted_iota(jnp.int32, (tq, tq), 0))
        new = []
        for h in range(heads):
            m, l, acc = state[h]
            kb = k_ref[pl.ds(start, tq), h * LANES:(h + 1) * LANES]
            vb = v_ref[pl.ds(start, tq), (h // 2) * LANES:(h // 2 + 1) * LANES]
            s = _dot_nt(qh[h], kb) * scale2
            if masked:
                s = jnp.where(causal, s, -jnp.inf)
            m_new = jnp.maximum(m, jnp.max(s, axis=-1, keepdims=True))
            a = jnp.exp2(m - m_new)
            p = jnp.exp2(s - m_new)
            l = l * a + jnp.sum(p, axis=-1, keepdims=True)
            acc = acc * a + _dot(p.astype(BF16), vb)
            new.append((m_new, l, acc))
        return tuple(new)

    init = tuple((jnp.full((tq, 1), -jnp.inf, F32), jnp.zeros((tq, 1), F32), jnp.zeros((tq, LANES), F32))
                 for _ in range(heads))
    state = block(i, init, True)
    state = lax.fori_loop(0, i, lambda t, s: block(i - 1 - t, s, False), state)
    for hp in range(heads // 2):
        e, o = state[2 * hp], state[2 * hp + 1]
        o_ref[:, hp * LANES:(hp + 1) * LANES] = jnp.where(first, e[2] / e[1], o[2] / o[1]).astype(o_ref.dtype)


def _mla_prompt_call(q, k, v, tq, scale):
    b, s, qw = q.shape
    vw = v.shape[2]
    heads = MLA_HEADS_PER_STEP
    groups = qw // (heads * LANES)
    gv = vw // groups
    return pl.pallas_call(
        functools.partial(_mla_prompt_kernel, tq, scale, heads),
        out_shape=jax.ShapeDtypeStruct((b, s, vw), BF16),
        grid=(b, groups, s // tq),
        in_specs=[
            pl.BlockSpec((None, tq, heads * LANES), lambda bi, hp, i: (bi, i, hp)),
            pl.BlockSpec((None, s, heads * LANES), lambda bi, hp, i: (bi, 0, hp)),
            pl.BlockSpec((None, s, gv), lambda bi, hp, i: (bi, 0, hp)),
        ],
        out_specs=pl.BlockSpec((None, tq, gv), lambda bi, hp, i: (bi, i, hp)),
        compiler_params=_cparams(("arbitrary", "arbitrary", "arbitrary")),
        name="mla_prompt",
    )(q, k, v)


def _token_masks(shape, t_new):
    qi = lax.broadcasted_iota(jnp.int32, shape, 0) % t_new
    ki = lax.broadcasted_iota(jnp.int32, shape, 1)
    return ki < qi, ki <= qi


def _sb_chunk(qm, u, kts, vts, carry, acc, t_new=None):
    zs = [_dot(qm, kt) for kt in kts]
    ls = [_neg_softplus(z) for z in zs]
    if t_new is not None:
        valid = _token_masks(zs[0].shape, t_new)[0]
        ls = [jnp.where(valid, l, 0.0) for l in ls]
    sfx = _suffix_sums(ls, u, True)
    for z, l, e, vt in zip(zs, ls, sfx, vts):
        w = jnp.exp(z + l + (e + carry))
        if t_new is not None:
            w = jnp.where(valid, w, 0.0)
        carry = carry + (e[:, :1] + l[:, :1])
        acc = acc + _dot_nt(w.astype(BF16), vt)
    return carry, acc


def _mla_chunk(ql, qp, scale2, cbs, rts, m, l, lat, t_new=None):
    ss = [(_dot_nt(ql, cb) + _dot(qp, rt)) * scale2 for cb, rt in zip(cbs, rts)]
    if t_new is not None:
        valid = _token_masks(ss[0].shape, t_new)[1]
        ss = [jnp.where(valid, s, -jnp.inf) for s in ss]
    m_new = jnp.maximum(m, jnp.max(functools.reduce(jnp.maximum, ss), axis=-1, keepdims=True))
    a = jnp.exp2(m - m_new)
    ps = [jnp.exp2(s - m_new) for s in ss]
    l = l * a + jnp.sum(functools.reduce(jnp.add, ps), axis=-1, keepdims=True)
    lat = lat * a
    for p, cb in zip(ps, cbs):
        lat = lat + _dot(p.astype(BF16), cb)
    return m_new, l, lat


def _load_pages(pages, g_pages, which, per_page=4):
    return [pages[per_page * g + which][...].astype(BF16) for g in range(g_pages)]


def _sample_head_kernel(cfg, pt_ref, *refs):
    g_pages, t_new, mla_scale = cfg
    qm_ref, ql_ref, qp_ref, kn_ref, vn_ref, cn_ref, rn_ref = refs[:7]
    pages = refs[7:7 + 4 * g_pages]
    carry_ref, acc_ref, m_ref, l_ref, lat_ref = refs[7 + 4 * g_pages:]
    u = _suffix_matrix(kn_ref.shape[1])
    qm, ql, qp = qm_ref[...], ql_ref[...], qp_ref[...]
    scale2 = mla_scale * LOG2E
    carry = jnp.zeros(carry_ref.shape, F32)
    acc = jnp.zeros(acc_ref.shape, F32)
    carry, acc = _sb_chunk(qm, u, [kn_ref[...]], [vn_ref[...]], carry, acc, t_new)
    carry, acc = _sb_chunk(qm, u, _load_pages(pages, g_pages, 0), _load_pages(pages, g_pages, 1), carry, acc)
    carry_ref[...] = carry
    acc_ref[...] = acc
    m = jnp.full(m_ref.shape, -jnp.inf, F32)
    l = jnp.zeros(l_ref.shape, F32)
    lat = jnp.zeros(lat_ref.shape, F32)
    m, l, lat = _mla_chunk(ql, qp, scale2, [cn_ref[...]], [rn_ref[...]], m, l, lat, t_new)
    m, l, lat = _mla_chunk(ql, qp, scale2, _load_pages(pages, g_pages, 2), _load_pages(pages, g_pages, 3), m, l, lat)
    m_ref[...] = m
    l_ref[...] = l
    lat_ref[...] = lat


def _fold_heads(x, n_heads, t_new):
    w = x.shape[1]
    own = (lax.broadcasted_iota(jnp.int32, x.shape, 1) // (w // n_heads)
           == lax.broadcasted_iota(jnp.int32, x.shape, 0) // t_new)
    return jnp.sum(jnp.where(own, x, 0.0).reshape(n_heads, t_new, w), axis=0)


def _sample_sb_tail_kernel(g_pages, pt_ref, dead_ref, *refs):
    qm_ref, carry0_ref, acc0_ref = refs[:3]
    pages = refs[3:3 + 2 * g_pages]
    out_ref = refs[3 + 2 * g_pages]
    carry_ref, acc_ref = refs[4 + 2 * g_pages:]
    b = pl.program_id(0)
    c = pl.program_id(1)

    @pl.when(c == 0)
    def _():
        carry_ref[...] = carry0_ref[...]
        acc_ref[...] = acc0_ref[...]

    @pl.when(dead_ref[b] == 0)
    def _():
        u = _suffix_matrix(pages[0].shape[1])
        carry, acc = _sb_chunk(qm_ref[...], u, _load_pages(pages, g_pages, 0, 2), _load_pages(pages, g_pages, 1, 2),
                               carry_ref[...], acc_ref[...])
        carry_ref[...] = carry
        acc_ref[...] = acc

    @pl.when(c == pl.num_programs(1) - 1)
    def _():
        out_ref[...] = acc_ref[...]


def _sample_sb_fold_kernel(n_heads, t_new, acc_ref, o_ref):
    for s in range(acc_ref.shape[0]):
        o_ref[s] = _fold_heads(acc_ref[s], n_heads, t_new)


def _sample_mla_tail_kernel(cfg, pt_ref, *refs):
    g_pages, n_heads, t_new, mla_scale = cfg
    ql_ref, qp_ref, wuv_ref, m0_ref, l0_ref, lat0_ref = refs[:6]
    pages = refs[6:6 + 2 * g_pages]
    yc_ref = refs[6 + 2 * g_pages]
    m_ref, l_ref, lat_ref = refs[7 + 2 * g_pages:]
    c = pl.program_id(1)

    @pl.when(c == 0)
    def _():
        m_ref[...] = m0_ref[...]
        l_ref[...] = l0_ref[...]
        lat_ref[...] = lat0_ref[...]

    m, l, lat = _mla_chunk(ql_ref[...], qp_ref[...], mla_scale * LOG2E, _load_pages(pages, g_pages, 0, 2),
                           _load_pages(pages, g_pages, 1, 2), m_ref[...], l_ref[...], lat_ref[...])
    m_ref[...] = m
    l_ref[...] = l
    lat_ref[...] = lat

    @pl.when(c == pl.num_programs(1) - 1)
    def _():
        lat_n = (lat_ref[...] / l_ref[...]).astype(BF16)
        yc_ref[...] = _fold_heads(_dot(lat_n, wuv_ref[...]), n_heads, t_new)


def _sample_attn_call(page_table, qm, ql, qp, kn, vn, cn, rn, wuv, caches, layer, t_new, mla_scale,
                      n_sb_heads, n_mla_heads, g_pages):
    nb, n_pages = page_table.shape
    sbw, kvr = caches[0].shape[2], caches[2].shape[3]
    vw = wuv.shape[1]
    assert n_pages >= 2
    head_pages = _pick_tile(n_pages // 2, g_pages)
    g_pages = _pick_tile(n_pages - head_pages, g_pages)
    steps = (n_pages - head_pages) // g_pages
    pt_flat = page_table.reshape(-1)
    rows_sb, rows_mla = qm.shape[1], ql.shape[1]
    state_shapes = [(rows_sb, 1), (rows_sb, sbw), (rows_mla, 1), (rows_mla, 1), (rows_mla, kvr)]
    sds = jax.ShapeDtypeStruct

    def seq1(a):
        return pl.BlockSpec((None,) + a.shape[1:], lambda b, pt: (b,) + (0,) * (a.ndim - 1))

    def head_page(cache, g):
        return pl.BlockSpec((None, None) + cache.shape[2:],
                            lambda b, pt: (layer, pt[b * n_pages + n_pages - 1 - g], 0, 0))

    head_in = [qm, ql, qp, kn, vn, cn, rn]
    in_specs = [seq1(a) for a in head_in]
    args = list(head_in)
    for g in range(head_pages):
        in_specs += [head_page(cache, g) for cache in caches]
        args += list(caches)
    state = pl.pallas_call(
        functools.partial(_sample_head_kernel, (head_pages, t_new, mla_scale)),
        out_shape=[sds((nb,) + s, F32) for s in state_shapes],
        grid_spec=pltpu.PrefetchScalarGridSpec(
            num_scalar_prefetch=1, grid=(nb,), in_specs=in_specs,
            out_specs=[pl.BlockSpec((None,) + s, lambda b, pt: (b, 0, 0)) for s in state_shapes]),
        compiler_params=_cparams(("arbitrary",)),
        name="sample_head",
    )(pt_flat, *args)

    dead = jnp.all(state[0] < SB_DEAD_LOG, axis=(1, 2)).astype(jnp.int32)

    def sb_tail(acc0):
        def seq2(a):
            return pl.BlockSpec((None,) + a.shape[1:], lambda b, c, pt, dd: (b,) + (0,) * (a.ndim - 1))

        def page_spec(cache, g):
            def index(b, c, pt, dd):
                p = pt[b * n_pages + n_pages - 1 - head_pages - (c * g_pages + g)]
                return (layer, jnp.where(dd[b] != 0, 0, p), 0, 0)
            return pl.BlockSpec((None, None) + cache.shape[2:], index)

        tail_in = [qm, state[0], acc0]
        in_specs = [seq2(a) for a in tail_in]
        args = list(tail_in)
        for g in range(g_pages):
            in_specs += [page_spec(cache, g) for cache in caches[:2]]
            args += list(caches[:2])
        return pl.pallas_call(
            functools.partial(_sample_sb_tail_kernel, g_pages),
            out_shape=sds((nb,) + state_shapes[1], F32),
            grid_spec=pltpu.PrefetchScalarGridSpec(
                num_scalar_prefetch=2, grid=(nb, steps), in_specs=in_specs,
                out_specs=pl.BlockSpec((None,) + state_shapes[1], lambda b, c, pt, dd: (b, 0, 0)),
                scratch_shapes=[pltpu.VMEM(s, F32) for s in state_shapes[:2]]),
            compiler_params=_cparams(("arbitrary", "arbitrary")),
            name="sample_sb_tail",
        )(pt_flat, dead, *args)

    acc = lax.cond(jnp.all(dead != 0), lambda a: a, sb_tail, state[1])
    fold = _pick_tile(nb, 16)
    ya = pl.pallas_call(
        functools.partial(_sample_sb_fold_kernel, n_sb_heads, t_new),
        out_shape=sds((nb, t_new, sbw), F32),
        grid=(nb // fold,),
        in_specs=[pl.BlockSpec((fold,) + state_shapes[1], lambda i: (i, 0, 0))],
        out_specs=pl.BlockSpec((fold, t_new, sbw), lambda i: (i, 0, 0)),
        compiler_params=_cparams(("arbitrary",)),
        name="sample_sb_fold",
    )(acc)

    m_pages = _largest_divisor(n_pages - head_pages, 2 * g_pages)

    def seq3(a):
        return pl.BlockSpec((None,) + a.shape[1:], lambda b, c, pt: (b,) + (0,) * (a.ndim - 1))

    def mla_page(cache, g):
        return pl.BlockSpec(
            (None, None) + cache.shape[2:],
            lambda b, c, pt: (layer, pt[b * n_pages + n_pages - 1 - head_pages - (c * m_pages + g)], 0, 0))

    in_specs = [seq3(ql), seq3(qp), pl.BlockSpec(wuv.shape, lambda b, c, pt: (0, 0))] + [seq3(a) for a in state[2:]]
    args = [ql, qp, wuv] + list(state[2:])
    for g in range(m_pages):
        in_specs += [mla_page(cache, g) for cache in caches[2:]]
        args += list(caches[2:])
    yc = pl.pallas_call(
        functools.partial(_sample_mla_tail_kernel, (m_pages, n_mla_heads, t_new, mla_scale)),
        out_shape=sds((nb, t_new, vw), F32),
        grid_spec=pltpu.PrefetchScalarGridSpec(
            num_scalar_prefetch=1, grid=(nb, (n_pages - head_pages) // m_pages), in_specs=in_specs,
            out_specs=pl.BlockSpec((None, t_new, vw), lambda b, c, pt: (b, 0, 0)),
            scratch_shapes=[pltpu.VMEM(s, F32) for s in state_shapes[2:]]),
        compiler_params=_cparams(("arbitrary", "arbitrary")),
        name="sample_mla_tail",
    )(pt_flat, *args)
    return ya, yc


def _conv_kernel(taps, u_ref, halo_ref, w_ref, b_ref, g_ref, beta_ref, o_ref, scr_ref):
    tt = u_ref.shape[1]
    scr_ref[:, :CONV_HALO, :] = halo_ref[...]
    scr_ref[:, CONV_HALO:, :] = u_ref[...]
    first = CONV_HALO - (taps - 1)
    y = jnp.zeros(u_ref.shape, F32)
    for j in range(taps):
        y = y + scr_ref[:, first + j:first + j + tt, :] * w_ref[j:j + 1, :]
    y = y + b_ref[...]
    mu = jnp.mean(y, axis=-1, keepdims=True)
    yc = y - mu
    var = jnp.mean(yc * yc, axis=-1, keepdims=True)
    y = yc * lax.rsqrt(var + EPS) * g_ref[...] + beta_ref[...]
    o_ref[...] = (y * _sigmoid(y)).astype(o_ref.dtype)


def _conv_call(u3, halo, w_dw, b_dw, g, beta, tb):
    nb, tt, cw = u3.shape
    taps = w_dw.shape[0]
    blk = lambda t: pl.BlockSpec((tb, t, cw), lambda i: (i, 0, 0))
    return pl.pallas_call(
        functools.partial(_conv_kernel, taps),
        out_shape=jax.ShapeDtypeStruct((nb, tt, cw), BF16),
        grid=(nb // tb,),
        in_specs=[blk(tt), blk(CONV_HALO), _const_spec((taps, cw)), _const_spec((1, cw)), _const_spec((1, cw)),
                  _const_spec((1, cw))],
        out_specs=blk(tt),
        scratch_shapes=[pltpu.VMEM((tb, CONV_HALO + tt, cw), F32)],
        compiler_params=_cparams(("arbitrary",)),
        name="conv_module",
    )(u3, halo, w_dw, b_dw, g, beta)


def _merge_kernel(x_ref, sh_ref, sc_ref, gt_ref, g1_ref, ya_ref, cact_ref, yc_ref, wg_ref, wsb_ref, wcv_ref,
                  wml_ref, wout_ref, o_ref):
    x = x_ref[...]
    d = x.shape[1]
    h = (_rms(x, g1_ref[...]) * (1.0 + sc_ref[...]) + sh_ref[...]).astype(BF16)
    branches = (
        _dot(ya_ref[...].astype(BF16), wsb_ref[...]),
        _dot(cact_ref[...].astype(BF16), wcv_ref[...]),
        _dot(yc_ref[...].astype(BF16), wml_ref[...]),
    )
    merged = jnp.zeros_like(x)
    for k, br in enumerate(branches):
        merged = merged + _sigmoid(_dot(h, wg_ref[:, k * d:(k + 1) * d])) * br
    o_ref[...] = x + gt_ref[...] * _dot(merged.astype(BF16), wout_ref[...])


def _merge_call(x, mod, g1, ya, cact, yc, lw, per_row, tm, tiles_per_seq):
    rows, d = x.shape
    row = lambda w: pl.BlockSpec((tm, w), lambda i: (i, 0))
    ws = [lw["wg"], lw["wsb"], lw["wcv"], lw["wml"], lw["wout"]]
    return pl.pallas_call(
        _merge_kernel,
        out_shape=jax.ShapeDtypeStruct((rows, d), F32),
        grid=(rows // tm,),
        in_specs=[row(d)] + [_mod_spec(per_row, tm, d, k, tiles_per_seq) for k in (0, 1, 2)]
        + [_const_spec((1, d)), row(ya.shape[1]), row(cact.shape[1]), row(yc.shape[1])]
        + [_const_spec(w.shape) for w in ws],
        out_specs=row(d),
        compiler_params=_cparams(("arbitrary",)),
        name="merge_out",
    )(x, mod, mod, mod, g1, ya, cact, yc, *ws)


def _ffn_kernel(chunk, final, x_ref, sh_ref, sc_ref, gt_ref, g2_ref, w1_ref, w2_ref, fg_ref, o_ref):
    x = x_ref[...]
    h = (_rms(x, g2_ref[...]) * (1.0 + sc_ref[...]) + sh_ref[...]).astype(BF16)
    acc = jnp.zeros_like(x)
    for c in range(w1_ref.shape[1] // chunk):
        t = jnp.maximum(_dot(h, w1_ref[:, c * chunk:(c + 1) * chunk]), 0.0)
        acc = acc + _dot((t * t).astype(BF16), w2_ref[c * chunk:(c + 1) * chunk, :])
    y = x + gt_ref[...] * acc
    if final:
        y = _rms(y, fg_ref[...])
    o_ref[...] = y


def _ffn_call(x, mod, g2, w1, w2, fg, final, per_row, tm, tiles_per_seq):
    rows, d = x.shape
    row = pl.BlockSpec((tm, d), lambda i: (i, 0))
    return pl.pallas_call(
        functools.partial(_ffn_kernel, 512, final),
        out_shape=jax.ShapeDtypeStruct((rows, d), F32),
        grid=(rows // tm,),
        in_specs=[row] + [_mod_spec(per_row, tm, d, k, tiles_per_seq) for k in (3, 4, 5)]
        + [_const_spec((1, d)), _const_spec(w1.shape), _const_spec(w2.shape), _const_spec((1, d))],
        out_specs=row,
        compiler_params=_cparams(("arbitrary",)),
        name="ffn_final" if final else "ffn",
    )(x, mod, mod, mod, g2, w1, w2, fg)


def _rot_cols(w):
    half = w.shape[-1] // 2
    return jnp.concatenate([-w[..., half:], w[..., :half]], axis=-1)


def _pad_cols(w, width):
    return jnp.pad(w, [(0, 0)] * (w.ndim - 1) + [(0, width - w.shape[-1])])


def _prep_layer(l, dims, w_in, w_o_sb, w_o_conv, w_uq, w_uk, w_uv, w_o_mla, w_out, w_ff1, w_ff2,
                q_norm_g, kv_norm_g):
    sbw, cw, qr, kvr, rope, _ = dims
    d = w_in.shape[1]
    nh, nope, vd = w_uk.shape[2], w_uk.shape[3], w_uv.shape[3]
    wi = w_in[l]
    o = 3 * sbw + 2 * cw + qr + kvr
    w_kpe = wi[:, o:o + rope]
    w1 = jnp.concatenate([wi[:, :o], _pad_cols(w_kpe, LANES), _pad_cols(_rot_cols(w_kpe), LANES)], axis=1)
    wg = wi[:, o + rope:]
    wq = w_uq[l].reshape(qr, nh, nope + rope)
    wq_nope, wq_pe = wq[..., :nope], wq[..., nope:]
    wq_rot = _rot_cols(wq_pe)
    zn = jnp.zeros_like(wq_nope)
    wqa_p = _pad_cols(jnp.concatenate([wq_nope, wq_pe], -1), LANES).reshape(qr, nh * LANES)
    wqb_p = _pad_cols(jnp.concatenate([zn, wq_rot], -1), LANES).reshape(qr, nh * LANES)
    wqa_s = jnp.concatenate([wq_nope.reshape(qr, -1), wq_pe.reshape(qr, -1)], axis=1)
    wqb_s = jnp.concatenate([zn.reshape(qr, -1), wq_rot.reshape(qr, -1)], axis=1)
    wk = _pad_cols(w_uk[l], LANES).reshape(kvr, nh * LANES)
    ek_head = jnp.concatenate([jnp.zeros((rope, nope), F32), jnp.eye(rope, dtype=F32),
                               jnp.zeros((rope, LANES - nope - rope), F32)], axis=1)
    ek = jnp.tile(ek_head, (1, nh))
    wv = w_uv[l].reshape(kvr, nh * vd)
    wabs = jnp.einsum('chd,hg->hdgc', w_uk[l], jnp.eye(nh, dtype=F32)).reshape(nh * nope, nh * kvr)
    b = lambda a: a.astype(BF16)
    return {
        "w1": b(w1), "wg": b(wg), "wqa_p": b(wqa_p), "wqb_p": b(wqb_p), "wqa_s": b(wqa_s), "wqb_s": b(wqb_s),
        "wk": b(wk), "ek": b(ek), "wv": b(wv), "wabs": b(wabs),
        "wsb": b(w_o_sb[l]), "wcv": b(w_o_conv[l]), "wml": b(w_o_mla[l]), "wout": b(w_out[l]),
        "wff1": b(w_ff1[l]), "wff2": b(w_ff2[l]),
        "qg": q_norm_g[l][None, :], "kvg": kv_norm_g[l][None, :],
    }


def _rope_tables(pos, rope, nh, nope, prompt_layout):
    half = rope // 2
    inv = ROPE_BASE ** (-jnp.arange(half, dtype=F32) / half)
    ang = pos.astype(F32)[:, None] * inv[None, :]
    cos = jnp.concatenate([jnp.cos(ang)] * 2, axis=1)
    sin = jnp.concatenate([jnp.sin(ang)] * 2, axis=1)
    n = pos.shape[0]
    if prompt_layout:
        tc = jnp.concatenate([jnp.ones((n, nope), F32), cos, jnp.zeros((n, LANES - nope - rope), F32)], axis=1)
        ts = jnp.concatenate([jnp.zeros((n, nope), F32), sin, jnp.zeros((n, LANES - nope - rope), F32)], axis=1)
        tc, ts = jnp.tile(tc, (1, nh)), jnp.tile(ts, (1, nh))
    else:
        tc = jnp.concatenate([jnp.ones((n, nh * nope), F32), jnp.tile(cos, (1, nh))], axis=1)
        ts = jnp.concatenate([jnp.zeros((n, nh * nope), F32), jnp.tile(sin, (1, nh))], axis=1)
    return tc, ts, cos, sin


def _largest_divisor(n, target):
    return max(t for t in range(1, min(n, target) + 1) if n % t == 0)


def _pick_tile(n, target):
    t = min(n, target)
    while n % t:
        t //= 2
    return t


def kernel(x_prompt, x_sample, c_prompt, c_sample, cache_sb_k, cache_sb_v, cache_mla_ckv, cache_mla_kpe,
           state_conv, page_table, norm1_g, norm2_g, final_g, w_ada, b_ada, w_in, w_o_sb, w_dw, b_dw, cln_g,
           cln_b, w_o_conv, q_norm_g, w_uq, kv_norm_g, w_uk, w_uv, w_o_mla, w_out, w_ff1, w_ff2):
    bsz, seq, d = x_prompt.shape
    nb, t_new, _ = x_sample.shape
    depth, n_pool, page, n_sb_heads, sb_dh = cache_sb_k.shape
    sbw = n_sb_heads * sb_dh
    cw, taps = w_dw.shape[2], w_dw.shape[1]
    qr, kvr, rope = w_uq.shape[1], w_uk.shape[1], cache_mla_kpe.shape[-1]
    nh, nope, vd = w_uk.shape[2], w_uk.shape[3], w_uv.shape[3]
    n_pages = page_table.shape[1]
    past_len = n_pages * page
    mla_scale = float((nope + rope) ** -0.5)
    dims = (sbw, cw, qr, kvr, rope, float(sb_dh ** -0.5))
    assert taps - 1 <= CONV_HALO and sbw % LANES == 0 and nope + rope <= LANES

    rows_p, rows_s = bsz * seq, nb * t_new
    tm_p = _pick_tile(seq, ROW_TILE)
    tm_s = _pick_tile(rows_s, ROW_TILE // 2)
    tq = _pick_tile(seq, ATTN_TILE)
    tiles_per_seq = seq // tm_p

    n_c = bsz + nb
    c_all = jnp.pad(jnp.concatenate([c_prompt, c_sample], axis=0), ((0, -n_c % 8), (0, 0)))
    mod = _mod_call(c_all, w_ada.astype(BF16), b_ada)
    mod_p = mod[:, :bsz, None, :]
    mod_s = jnp.repeat(mod[:, bsz:n_c], t_new, axis=1)

    pos_p = jnp.arange(seq, dtype=jnp.int32)
    pos_s = past_len + jnp.arange(t_new, dtype=jnp.int32)
    tabs_p = _rope_tables(pos_p, rope, nh, nope, True)
    tabs_s = tuple(jnp.tile(t, (nb, 1)) for t in _rope_tables(pos_s, rope, nh, nope, False))

    caches = (cache_sb_k.transpose(0, 1, 3, 4, 2).reshape(depth, n_pool, sbw, page),
              cache_sb_v.transpose(0, 1, 3, 4, 2).reshape(depth, n_pool, sbw, page),
              cache_mla_ckv, cache_mla_kpe.transpose(0, 1, 3, 2))
    head_of_col = jnp.arange(sbw, dtype=jnp.int32) // sb_dh
    head_mask = (head_of_col[None, :] == jnp.arange(n_sb_heads, dtype=jnp.int32)[:, None])

    xp = x_prompt.reshape(rows_p, d)
    xs = x_sample.reshape(rows_s, d)
    rows_out_p, rows_out_s = [], []
    for l in range(depth):
        lw = _prep_layer(l, dims, w_in, w_o_sb, w_o_conv, w_uq, w_uk, w_uv, w_o_mla, w_out, w_ff1, w_ff2,
                         q_norm_g, kv_norm_g)
        g1, g2 = norm1_g[l][None, :], norm2_g[l][None, :]
        conv_w = (w_dw[l], b_dw[l][None, :], cln_g[l][None, :], cln_b[l][None, :])
        final = l == depth - 1
        fg = final_g[None, :]

        lw_p = dict(lw, wqa=lw["wqa_p"], wqb=lw["wqb_p"])
        (qab, kab, vab, ka, va, u, q, kf, vc, ckv, kpe) = _in_proj_call(
            xp, mod_p[l], g1, lw_p, tabs_p, dims, True, tm_p, tiles_per_seq)
        r3 = lambda a: a.reshape(bsz, seq, a.shape[-1])
        ya = _sb_prompt_call(r3(qab), r3(kab), r3(vab), tq).reshape(rows_p, sbw)
        yc = _mla_prompt_call(r3(q), r3(kf), r3(vc), tq, mla_scale).reshape(rows_p, nh * vd)
        u4 = u.reshape(bsz, tiles_per_seq, tm_p, cw)
        halo = jnp.concatenate([jnp.zeros((bsz, 1, CONV_HALO, cw), F32), u4[:, :-1, tm_p - CONV_HALO:, :]], axis=1)
        cact = _conv_call(u4.reshape(bsz * tiles_per_seq, tm_p, cw), halo.reshape(bsz * tiles_per_seq, CONV_HALO, cw),
                          *conv_w, 1).reshape(rows_p, cw)
        xp = _merge_call(xp, mod_p[l], g1, ya, cact, yc, lw, False, tm_p, tiles_per_seq)
        xp = _ffn_call(xp, mod_p[l], g2, lw["wff1"], lw["wff2"], fg, final, False, tm_p, tiles_per_seq)
        rows_out_p.append((ka.reshape(bsz, seq, n_sb_heads, sb_dh), va.reshape(bsz, seq, n_sb_heads, sb_dh),
                           ckv.reshape(bsz, seq, kvr), kpe.reshape(bsz, seq, rope),
                           u.reshape(bsz, seq, cw)[:, seq - (taps - 1):, :]))

        lw_s = dict(lw, wqa=lw["wqa_s"], wqb=lw["wqb_s"])
        (qab, ka, va, u, qlat, qpe, ckv, kpe) = _in_proj_call(
            xs, mod_s[l], g1, lw_s, tabs_s, dims, False, tm_s, 1)
        qm = jnp.where(head_mask[None, :, None, :], qab.reshape(nb, 1, t_new, sbw), 0).reshape(
            nb, n_sb_heads * t_new, sbw).astype(BF16)
        ql = qlat.reshape(nb, t_new, nh, kvr).transpose(0, 2, 1, 3).reshape(nb, nh * t_new, kvr)
        qp = qpe.reshape(nb, t_new, nh, rope).transpose(0, 2, 1, 3).reshape(nb, nh * t_new, rope)
        padk = lambda a: jnp.pad(a.reshape(nb, t_new, a.shape[-1]).astype(BF16), ((0, 0), (0, page - t_new), (0, 0)))
        padt = lambda a: padk(a).transpose(0, 2, 1)
        ya, yc = _sample_attn_call(page_table, qm, ql, qp, padt(ka), padt(va), padk(ckv), padt(kpe), lw["wv"],
                                   caches, l, t_new, mla_scale, n_sb_heads, nh, PAGES_PER_STEP)
        u3 = u.reshape(nb, t_new, cw)
        state = state_conv[l]
        halo = jnp.pad(state, ((0, 0), (CONV_HALO - (taps - 1), 0), (0, 0)))
        cact = _conv_call(u3, halo, *conv_w, _pick_tile(nb, 32)).reshape(rows_s, cw)
        xs = _merge_call(xs, mod_s[l], g1, ya.reshape(rows_s, sbw), cact, yc.reshape(rows_s, nh * vd), lw, True,
                         tm_s, 1)
        xs = _ffn_call(xs, mod_s[l], g2, lw["wff1"], lw["wff2"], fg, final, True, tm_s, 1)
        new_state = jnp.concatenate([state, u3], axis=1)[:, -(taps - 1):, :]
        rows_out_s.append((ka.reshape(nb, t_new, n_sb_heads, sb_dh), va.reshape(nb, t_new, n_sb_heads, sb_dh),
                           ckv.reshape(nb, t_new, kvr), kpe.reshape(nb, t_new, rope), new_state))

    stk = lambda rows, i: jnp.stack([r[i] for r in rows], axis=0)
    return (xp.reshape(bsz, seq, d), xs.reshape(nb, t_new, d),
            stk(rows_out_p, 0), stk(rows_out_p, 1), stk(rows_out_p, 2), stk(rows_out_p, 3), stk(rows_out_p, 4),
            stk(rows_out_s, 0), stk(rows_out_s, 1), stk(rows_out_s, 2), stk(rows_out_s, 3), stk(rows_out_s, 4))
```
